```python
import jax
import jax.numpy as jnp
from jax import lax
import numpy as np

D_MODEL = 4096
BATCH = 2
SEQ = 8192
DEPTH = 2

GRID_W = 64
CTX_LEN = 256

N_FOURIER_GROUPS = 4
FOURIER_GROUP_DIM = D_MODEL // 8
FOURIER_WIDTH = N_FOURIER_GROUPS * FOURIER_GROUP_DIM

GLA_HEADS = 4
GLA_DK = D_MODEL // 16
GLA_DV = D_MODEL // 8
GLA_KEY = GLA_HEADS * GLA_DK
GLA_VAL = GLA_HEADS * GLA_DV
GATE_RANK = 16
GATE_TAU = 16.0
CHUNK = 64
N_DIR = 2

N_BRANCHES = 2
N_EXPERTS = 16
EXPERT_FF = D_MODEL // 4
CAPACITY_FACTOR = 2
N_MOD = 6
EPS = 1e-6

SPLIT_SIZES = (FOURIER_WIDTH, GLA_KEY, GLA_KEY, GLA_VAL, GLA_VAL, N_DIR * GATE_RANK, N_BRANCHES * D_MODEL)
IN_WIDTH = sum(SPLIT_SIZES)

kernel_name = "hybrid_fourier_gla_ecmoe_dit"


def _rmsnorm(x, g):
    xf = x.astype(jnp.float32)
    y = xf * lax.rsqrt(jnp.mean(xf * xf, axis=-1, keepdims=True) + EPS)
    return (y * g.astype(jnp.float32)).astype(x.dtype)


def _split_projection(z):
    offsets = np.cumsum(SPLIT_SIZES)[:-1].tolist()
    return jnp.split(z, offsets, axis=-1)


def _fourier_mix(u):
    b, n, _ = u.shape
    ug = u.astype(jnp.float32).reshape(b, n, N_FOURIER_GROUPS, FOURIER_GROUP_DIM)
    f = jnp.fft.fft2(ug, axes=(1, 3), norm='ortho')
    return jnp.real(f).reshape(b, n, FOURIER_WIDTH).astype(u.dtype)


def _gla_direction(q, k, v, logg, s0):
    b, h, n, dk = q.shape
    dv = v.shape[-1]
    nc = n // CHUNK
    q = q.reshape(b, h, nc, CHUNK, dk)
    k = k.reshape(b, h, nc, CHUNK, dk)
    logg = logg.reshape(b, h, nc, CHUNK, dk)
    v = v.reshape(b, h, nc, CHUNK, dv)
    cum = jnp.cumsum(logg, axis=3)
    total = cum[:, :, :, -1, :]
    q_dec = q * jnp.exp(cum)
    k_inv = k * jnp.exp(-cum)
    k_tail = k * jnp.exp(total[:, :, :, None, :] - cum)
    lower = jnp.tril(jnp.ones((CHUNK, CHUNK), dtype=bool))
    att = jnp.where(lower, jnp.einsum('bhcid,bhcjd->bhcij', q_dec, k_inv), 0.0)
    o_intra = jnp.einsum('bhcij,bhcjv->bhciv', att, v)

    def step(s, inp):
        qd, kt, vc, tot = inp
        o = jnp.einsum('bhld,bhdv->bhlv', qd, s)
        s = s * jnp.exp(tot)[..., None] + jnp.einsum('bhld,bhlv->bhdv', kt, vc)
        return s, o

    xs = (jnp.moveaxis(q_dec, 2, 0), jnp.moveaxis(k_tail, 2, 0), jnp.moveaxis(v, 2, 0), jnp.moveaxis(total, 2, 0))
    s_final, o_inter = lax.scan(step, s0, xs)
    o = o_intra + jnp.moveaxis(o_inter, 0, 2)
    return o.reshape(b, h, n, dv), s_final


def _gla_mixer(q, k, v, r, lowrank, w_gate2, b_gate, g_head, s0_f, s0_b):
    b, n, _ = q.shape
    f32 = jnp.float32

    def heads(t, dh):
        return t.astype(f32).reshape(b, n, GLA_HEADS, dh).transpose(0, 2, 1, 3)

    qh = heads(q, GLA_DK) * (GLA_DK ** -0.5)
    kh = heads(k, GLA_DK)
    vh = heads(v, GLA_DV)
    lr = lowrank.astype(f32).reshape(b, n, N_DIR, GATE_RANK)
    gate_logit = jnp.einsum('bnzr,zrk->bnzk', lr, w_gate2.astype(f32)) + b_gate.astype(f32)
    logg = jax.nn.log_sigmoid(gate_logit) / GATE_TAU
    logg_f = heads(logg[:, :, 0], GLA_DK)
    logg_b = heads(logg[:, :, 1], GLA_DK)
    o_f, s_f = _gla_direction(qh, kh, vh, logg_f, s0_f)
    flip = lambda t: jnp.flip(t, axis=2)
    o_b, s_b = _gla_direction(flip(qh), flip(kh), flip(vh), flip(logg_b), s0_b)
    o = o_f + flip(o_b)
    o = o * lax.rsqrt(jnp.mean(o * o, axis=-1, keepdims=True) + EPS)
    o = o.transpose(0, 2, 1, 3).reshape(b, n, GLA_VAL) * g_head.astype(f32)
    o = o * jax.nn.silu(r.astype(f32))
    return o.astype(q.dtype), s_f, s_b


def _merge_branches(u_fourier, o_gla, gate_logits, w_branch_a, w_branch_b, w_out):
    ya = _fourier_mix(u_fourier) @ w_branch_a
    yb = o_gla @ w_branch_b
    ga, gb = jnp.split(gate_logits, N_BRANCHES, axis=-1)
    return (jax.nn.sigmoid(ga) * ya + jax.nn.sigmoid(gb) * yb) @ w_out


def _expert_choice_ffn(h, w_router, w_g, w_u, w_d):
    b, n, d = h.shape
    cap = CAPACITY_FACTOR * n // N_EXPERTS
    logits = jnp.einsum('bnd,de->bne', h, w_router).astype(jnp.float32)
    aff = jax.nn.softmax(logits, axis=-1)
    gates, idx = lax.top_k(jnp.swapaxes(aff, 1, 2), cap)
    xin = jax.vmap(lambda hb, ib: hb[ib])(h, idx)
    hid = jax.nn.silu(jnp.einsum('becd,edf->becf', xin, w_g)) * jnp.einsum('becd,edf->becf', xin, w_u)
    yout = jnp.einsum('becf,efd->becd', hid, w_d) * gates[..., None].astype(h.dtype)
    return jax.vmap(lambda ib, yb: jnp.zeros((n, d), yb.dtype).at[ib.reshape(-1)].add(yb.reshape(-1, d)))(idx, yout)


def setup_inputs(seed: int = 0) -> dict:
    key = jax.random.key(seed)
    ks = jax.random.split(key, 22)
    f32 = jnp.float32
    D = D_MODEL

    def nrm(k, shape, scale):
        return jax.random.normal(k, shape, f32) * scale

    return {
        'x': nrm(ks[0], (BATCH, SEQ, D), 1.0),
        'c': nrm(ks[1], (BATCH, D), 1.0),
        'ctx': nrm(ks[2], (BATCH, CTX_LEN, D), 1.0),
        'c_ctx': nrm(ks[3], (D,), 1.0),
        'w_ada': nrm(ks[4], (DEPTH, D, N_MOD * D), 0.5 * D ** -0.5),
        'b_ada': nrm(ks[5], (DEPTH, N_MOD * D), 0.02),
        'g_norm1': 1.0 + nrm(ks[6], (DEPTH, D), 0.02),
        'w_in': nrm(ks[7], (DEPTH, D, IN_WIDTH), D ** -0.5),
        'w_gate2': nrm(ks[8], (DEPTH, N_DIR, GATE_RANK, GLA_KEY), GATE_RANK ** -0.5),
        'b_gate': nrm(ks[9], (DEPTH, N_DIR, GLA_KEY), 0.5),
        'g_head': 1.0 + nrm(ks[10], (DEPTH, GLA_VAL), 0.02),
        'w_branch_a': nrm(ks[11], (DEPTH, FOURIER_WIDTH, D), FOURIER_WIDTH ** -0.5),
        'w_branch_b': nrm(ks[12], (DEPTH, GLA_VAL, D), GLA_VAL ** -0.5),
        'w_out': nrm(ks[13], (DEPTH, D, D), D ** -0.5),
        'g_norm2': 1.0 + nrm(ks[14], (DEPTH, D), 0.02),
        'w_router': nrm(ks[15], (DEPTH, D, N_EXPERTS), D ** -0.5),
        'w_exp_gate': nrm(ks[16], (DEPTH, N_EXPERTS, D, EXPERT_FF), D ** -0.5),
        'w_exp_up': nrm(ks[17], (DEPTH, N_EXPERTS, D, EXPERT_FF), D ** -0.5),
        'w_exp_down': nrm(ks[18], (DEPTH, N_EXPERTS, EXPERT_FF, D), EXPERT_FF ** -0.5),
        'g_final': 1.0 + nrm(ks[19], (D,), 0.02),
    }


def reference(x, c, ctx, c_ctx, w_ada, b_ada, g_norm1, w_in, w_gate2, b_gate, g_head,
              w_branch_a, w_branch_b, w_out, g_norm2, w_router, w_exp_gate, w_exp_up,
              w_exp_down, g_final):
    xl = x
    xc = ctx
    bsz = x.shape[0]
    zero_state = jnp.zeros((bsz, GLA_HEADS, GLA_DK, GLA_DV), jnp.float32)
    for l in range(DEPTH):
        last = l == DEPTH - 1
        mod_l = (jax.nn.silu(c) @ w_ada[l] + b_ada[l])[:, None, :]
        mod_c = (jax.nn.silu(c_ctx) @ w_ada[l] + b_ada[l]).reshape(1, 1, N_MOD * D_MODEL)
        sh1_l, sc1_l, gt1_l, sh2_l, sc2_l, gt2_l = jnp.split(mod_l, N_MOD, axis=-1)
        sh1_c, sc1_c, gt1_c, sh2_c, sc2_c, gt2_c = jnp.split(mod_c, N_MOD, axis=-1)

        hc = _rmsnorm(xc, g_norm1[l]) * (1.0 + sc1_c) + sh1_c
        hl = _rmsnorm(xl, g_norm1[l]) * (1.0 + sc1_l) + sh1_l
        four_c, q_c, k_c, v_c, r_c, lr_c, gl_c = _split_projection(hc @ w_in[l])
        four_l, q_l, k_l, v_l, r_l, lr_l, gl_l = _split_projection(hl @ w_in[l])
        gla_c, s_f, s_b = _gla_mixer(q_c, k_c, v_c, r_c, lr_c, w_gate2[l], b_gate[l], g_head[l],
                                     zero_state, zero_state)
        gla_l, _, _ = _gla_mixer(q_l, k_l, v_l, r_l, lr_l, w_gate2[l], b_gate[l], g_head[l], s_f, s_b)
        xl = xl + gt1_l * _merge_branches(four_l, gla_l, gl_l, w_branch_a[l], w_branch_b[l], w_out[l])
        if not last:
            xc = xc + gt1_c * _merge_branches(four_c, gla_c, gl_c, w_branch_a[l], w_branch_b[l], w_out[l])

        hl2 = _rmsnorm(xl, g_norm2[l]) * (1.0 + sc2_l) + sh2_l
        xl = xl + gt2_l * _expert_choice_ffn(hl2, w_router[l], w_exp_gate[l], w_exp_up[l], w_exp_down[l])
        if not last:
            hc2 = _rmsnorm(xc, g_norm2[l]) * (1.0 + sc2_c) + sh2_c
            xc = xc + gt2_c * _expert_choice_ffn(hc2, w_router[l], w_exp_gate[l], w_exp_up[l], w_exp_down[l])
    return _rmsnorm(xl, g_final)
```

```python
import functools
import math

import jax
import jax.numpy as jnp
from jax import lax
from jax.experimental import pallas as pl
from jax.experimental.pallas import tpu as pltpu

F32 = jnp.float32
BF16 = jnp.bfloat16

N_FOURIER_GROUPS = 4
GLA_HEADS = 4
GATE_TAU = 16.0
CHUNK = 64
N_DIR = 2
CAPACITY_FACTOR = 2
N_MOD = 6
EPS = 1e-6

DFT_INNER = 128
VMEM_LIMIT = 56 << 20
GLA_ROWS = 256
DMA_ROWS = 128


def _cparams(n_axes):
    return pltpu.CompilerParams(dimension_semantics=("arbitrary",) * n_axes,
                                vmem_limit_bytes=VMEM_LIMIT)


def _sigmoid(v):
    return 1.0 / (1.0 + jnp.exp(-v))


def _ada_kernel(c_ref, w_ref, b_ref, o_ref):
    c = c_ref[...]
    a = (c * _sigmoid(c)).astype(BF16)
    o_ref[...] = jnp.dot(a, w_ref[...].astype(BF16), preferred_element_type=F32) + b_ref[...]


def _adaln(c_rows, w_ada, b_ada, tn=512):
    depth, d, n = w_ada.shape
    r = c_rows.shape[0]
    return pl.pallas_call(
        _ada_kernel,
        grid=(depth, n // tn),
        in_specs=[pl.BlockSpec((r, d), lambda l, j: (0, 0)),
                  pl.BlockSpec((None, d, tn), lambda l, j: (l, 0, j)),
                  pl.BlockSpec((None, 1, tn), lambda l, j: (l, 0, j))],
        out_specs=pl.BlockSpec((None, r, tn), lambda l, j: (l, 0, j)),
        out_shape=jax.ShapeDtypeStruct((depth, r, n), F32),
        compiler_params=_cparams(2),
        name="adaln",
    )(c_rows, w_ada, b_ada.reshape(depth, 1, n))


def _normed(x_ref, g_ref, sc_ref, sh_ref):
    x = x_ref[...]
    y = x * lax.rsqrt(jnp.mean(x * x, axis=-1, keepdims=True) + EPS) * g_ref[...]
    return y * (1.0 + sc_ref[...]) + sh_ref[...]


def _norm_kernel(x_ref, g_ref, sc_ref, sh_ref, o_ref):
    o_ref[...] = _normed(x_ref, g_ref, sc_ref, sh_ref).astype(o_ref.dtype)


def _norm_router_kernel(x_ref, g_ref, sc_ref, sh_ref, wr_ref, o_ref, aff_ref):
    h = _normed(x_ref, g_ref, sc_ref, sh_ref)
    o_ref[...] = h
    logits = lax.dot_general(wr_ref[...], h.astype(BF16), (((1,), (1,)), ((), ())),
                             preferred_element_type=F32)
    e = jnp.exp(logits - jnp.max(logits, axis=0, keepdims=True))
    aff_ref[...] = e / jnp.sum(e, axis=0, keepdims=True)


def _seg_of(i, tm, n_lat, n_seg):
    return jnp.minimum((i * tm) // n_lat, n_seg)


def _norm_mod(x, g, mod, k_scale, k_shift, n_lat, n_seg, out_dtype, tm=256):
    t, d = x.shape
    seg = functools.partial(_seg_of, tm=tm, n_lat=n_lat, n_seg=n_seg)
    return pl.pallas_call(
        _norm_kernel,
        grid=(t // tm,),
        in_specs=[pl.BlockSpec((tm, d), lambda i: (i, 0)),
                  pl.BlockSpec((1, d), lambda i: (0, 0)),
                  pl.BlockSpec((None, 1, d), lambda i: (seg(i) * N_MOD + k_scale, 0, 0)),
                  pl.BlockSpec((None, 1, d), lambda i: (seg(i) * N_MOD + k_shift, 0, 0))],
        out_specs=pl.BlockSpec((tm, d), lambda i: (i, 0)),
        out_shape=jax.ShapeDtypeStruct((t, d), out_dtype),
        compiler_params=_cparams(1),
        name="norm_mod",
    )(x, g, mod, mod)


def _norm_mod_router(x, g, mod, k_scale, k_shift, wr_t, n_lat, n_seg, tm=256):
    t, d = x.shape
    e = wr_t.shape[0]
    seg = functools.partial(_seg_of, tm=tm, n_lat=n_lat, n_seg=n_seg)
    return pl.pallas_call(
        _norm_router_kernel,
        grid=(t // tm,),
        in_specs=[pl.BlockSpec((tm, d), lambda i: (i, 0)),
                  pl.BlockSpec((1, d), lambda i: (0, 0)),
                  pl.BlockSpec((None, 1, d), lambda i: (seg(i) * N_MOD + k_scale, 0, 0)),
                  pl.BlockSpec((None, 1, d), lambda i: (seg(i) * N_MOD + k_shift, 0, 0)),
                  pl.BlockSpec((e, d), lambda i: (0, 0))],
        out_specs=[pl.BlockSpec((tm, d), lambda i: (i, 0)),
                   pl.BlockSpec((e, tm), lambda i: (0, i))],
        out_shape=[jax.ShapeDtypeStruct((t, d), F32),
                   jax.ShapeDtypeStruct((e, t), F32)],
        compiler_params=_cparams(1),
        name="norm_mod_router",
    )(x, g, mod, mod, wr_t)


def _final_norm_kernel(x_ref, g_ref, o_ref):
    x = x_ref[...]
    o_ref[...] = x * lax.rsqrt(jnp.mean(x * x, axis=-1, keepdims=True) + EPS) * g_ref[...]


def _final_norm(x, g, rows, tm=256):
    d = x.shape[1]
    return pl.pallas_call(
        _final_norm_kernel,
        grid=(rows // tm,),
        in_specs=[pl.BlockSpec((tm, d), lambda i: (i, 0)),
                  pl.BlockSpec((1, d), lambda i: (0, 0))],
        out_specs=pl.BlockSpec((tm, d), lambda i: (i, 0)),
        out_shape=jax.ShapeDtypeStruct((rows, d), F32),
        compiler_params=_cparams(1),
        name="final_norm",
    )(x, g)


def _cache_weight(w_ref, ws_ref, first):
    @pl.when(first)
    def _():
        ws_ref[...] = w_ref[...].astype(BF16)


def _mm_kernel(a_ref, b_ref, o_ref, bs_ref):
    _cache_weight(b_ref, bs_ref, pl.program_id(1) == 0)
    o_ref[...] = jnp.dot(a_ref[...], bs_ref[...], preferred_element_type=F32).astype(o_ref.dtype)


def _layer_spec(block, index_map, w, l):
    if w.ndim == len(block):
        return pl.BlockSpec(block, index_map)
    return pl.BlockSpec((None,) + block, lambda *g: (l,) + index_map(*g))


def _matmul(a, b, l, col0, ncols, out_dtype=F32, tm=512, tn=512):
    t, k = a.shape
    tn = min(tn, ncols)
    c0 = col0 // tn
    return pl.pallas_call(
        _mm_kernel,
        grid=(ncols // tn, t // tm),
        in_specs=[pl.BlockSpec((tm, k), lambda j, i: (i, 0)),
                  _layer_spec((k, tn), lambda j, i: (0, c0 + j), b, l)],
        out_specs=pl.BlockSpec((tm, tn), lambda j, i: (i, j)),
        out_shape=jax.ShapeDtypeStruct((t, ncols), out_dtype),
        scratch_shapes=[pltpu.VMEM((k, tn), BF16)],
        compiler_params=_cparams(2),
        name="matmul",
    )(a, b)


def _merge_kernel(fl_ref, fc_ref, ol_ref, oc_ref, wa_ref, wb_ref, ga_ref, gb_ref, o_ref,
                  was_ref, wbs_ref, *, lat_tiles):
    i = pl.program_id(1)
    _cache_weight(wa_ref, was_ref, i == 0)
    _cache_weight(wb_ref, wbs_ref, i == 0)

    def compute(fa_ref, oa_ref):
        ya = jnp.dot(fa_ref[...], was_ref[...], preferred_element_type=F32)
        yb = jnp.dot(oa_ref[...], wbs_ref[...], preferred_element_type=F32)
        o_ref[...] = (_sigmoid(ga_ref[...]) * ya + _sigmoid(gb_ref[...]) * yb).astype(o_ref.dtype)

    @pl.when(i < lat_tiles)
    def _():
        compute(fl_ref, ol_ref)

    @pl.when(i >= lat_tiles)
    def _():
        compute(fc_ref, oc_ref)


def _merge(four_lat, four_ctx, ogla_lat, ogla_ctx, w_a, w_b, l, gates, tm=512, tn=512):
    ka, kb = four_lat.shape[1], ogla_lat.shape[1]
    lat_tiles = four_lat.shape[0] // tm
    t = gates.shape[0]
    d = w_a.shape[-1]
    nj = d // tn

    def lat(j, i):
        return (jnp.minimum(i, lat_tiles - 1), 0)

    def ctx(j, i):
        return (jnp.maximum(i - lat_tiles, 0), 0)

    return pl.pallas_call(
        functools.partial(_merge_kernel, lat_tiles=lat_tiles),
        grid=(nj, t // tm),
        in_specs=[pl.BlockSpec((tm, ka), lat),
                  pl.BlockSpec((tm, ka), ctx),
                  pl.BlockSpec((tm, kb), lat),
                  pl.BlockSpec((tm, kb), ctx),
                  _layer_spec((ka, tn), lambda j, i: (0, j), w_a, l),
                  _layer_spec((kb, tn), lambda j, i: (0, j), w_b, l),
                  pl.BlockSpec((tm, tn), lambda j, i: (i, j)),
                  pl.BlockSpec((tm, tn), lambda j, i: (i, nj + j))],
        out_specs=pl.BlockSpec((tm, tn), lambda j, i: (i, j)),
        out_shape=jax.ShapeDtypeStruct((t, d), BF16),
        scratch_shapes=[pltpu.VMEM((ka, tn), BF16), pltpu.VMEM((kb, tn), BF16)],
        compiler_params=_cparams(2),
        name="merge",
    )(four_lat, four_ctx, ogla_lat, ogla_ctx, w_a, w_b, gates, gates)


def _out_res_kernel(a_ref, b_ref, x_ref, gt_ref, o_ref, bs_ref):
    _cache_weight(b_ref, bs_ref, pl.program_id(1) == 0)
    y = jnp.dot(a_ref[...], bs_ref[...], preferred_element_type=F32)
    o_ref[...] = x_ref[...] + gt_ref[...] * y


def _out_residual(m, w_out, l, x, mod, k_gate, n_lat, n_seg, tm=512, tn=512):
    t, k = m.shape
    d = w_out.shape[-1]
    seg = functools.partial(_seg_of, tm=tm, n_lat=n_lat, n_seg=n_seg)
    return pl.pallas_call(
        _out_res_kernel,
        grid=(d // tn, t // tm),
        in_specs=[pl.BlockSpec((tm, k), lambda j, i: (i, 0)),
                  _layer_spec((k, tn), lambda j, i: (0, j), w_out, l),
                  pl.BlockSpec((tm, tn), lambda j, i: (i, j)),
                  pl.BlockSpec((None, 1, tn), lambda j, i: (seg(i) * N_MOD + k_gate, 0, j))],
        out_specs=pl.BlockSpec((tm, tn), lambda j, i: (i, j)),
        out_shape=jax.ShapeDtypeStruct((t, d), F32),
        scratch_shapes=[pltpu.VMEM((k, tn), BF16)],
        compiler_params=_cparams(2),
        name="out_residual",
    )(m, w_out, x, mod)


def _chan_dft_kernel(u_ref, w_ref, p_ref, q_ref):
    c = p_ref.shape[1]
    y = jnp.dot(u_ref[...].astype(BF16), w_ref[...], preferred_element_type=F32)
    p_ref[...] = y[:, :c]
    q_ref[...] = y[:, c:]


def _chan_dft(z, wc, width, tm=512):
    t = z.shape[0]
    c = wc.shape[0]
    return pl.pallas_call(
        _chan_dft_kernel,
        grid=(t // tm, width // c),
        in_specs=[pl.BlockSpec((tm, c), lambda i, g: (i, g)),
                  pl.BlockSpec((c, 2 * c), lambda i, g: (0, 0))],
        out_specs=[pl.BlockSpec((tm, c), lambda i, g: (i, g)),
                   pl.BlockSpec((tm, c), lambda i, g: (i, g))],
        out_shape=[jax.ShapeDtypeStruct((t, width), F32)] * 2,
        compiler_params=_cparams(2),
        name="chan_dft",
    )(z, wc)


def _seq_dft_kernel(p_ref, q_ref, m1_ref, m2_ref, o_ref, bs_ref, os_ref, *, n1, scale):
    n2 = DFT_INNER

    def stage1(i, carry):
        xp = p_ref[pl.ds(i, n2, stride=n1), :]
        xq = q_ref[pl.ds(i, n2, stride=n1), :]
        xs = jnp.concatenate([xp, xq], axis=0).astype(BF16)
        row0 = pl.multiple_of(i * (2 * n2), 2 * n2)
        bs_ref[pl.ds(row0, 2 * n2), :] = jnp.dot(m1_ref[i], xs, preferred_element_type=F32)
        return carry

    lax.fori_loop(0, n1, stage1, 0)

    def stage2(k2, carry):
        are = bs_ref[pl.ds(k2, n1, stride=2 * n2), :]
        aim = bs_ref[pl.ds(n2 + k2, n1, stride=2 * n2), :]
        a = jnp.concatenate([are, aim], axis=0).astype(BF16)
        y = jnp.dot(m2_ref[...], a, preferred_element_type=F32) * scale
        os_ref[pl.ds(k2, n1, stride=n2), :] = y
        return carry

    lax.fori_loop(0, n2, stage2, 0)
    o_ref[...] = os_ref[...].astype(o_ref.dtype)


def _seq_dft(p, q, m1, m2, n_seq, n_seg, scale, ft=128):
    t, w = p.shape
    n1 = n_seq // DFT_INNER
    kern = functools.partial(_seq_dft_kernel, n1=n1, scale=scale)
    return pl.pallas_call(
        kern,
        grid=(n_seg, w // ft),
        in_specs=[pl.BlockSpec((n_seq, ft), lambda b, j: (b, j)),
                  pl.BlockSpec((n_seq, ft), lambda b, j: (b, j)),
                  pl.BlockSpec(memory_space=pltpu.VMEM),
                  pl.BlockSpec(memory_space=pltpu.VMEM)],
        out_specs=pl.BlockSpec((n_seq, ft), lambda b, j: (b, j)),
        out_shape=jax.ShapeDtypeStruct((n_seg * n_seq, w), BF16),
        scratch_shapes=[pltpu.VMEM((n1 * 2 * DFT_INNER, ft), F32), pltpu.VMEM((n_seq, ft), F32)],
        compiler_params=_cparams(2),
        name="seq_dft",
    )(p, q, m1, m2)


def _ctx_dft_kernel(p_ref, q_ref, m_ref, o_ref, *, scale):
    a = jnp.concatenate([p_ref[...], q_ref[...]], axis=0).astype(BF16)
    o_ref[...] = (jnp.dot(m_ref[...], a, preferred_element_type=F32) * scale).astype(o_ref.dtype)


def _ctx_dft(p, q, mc, row_blk0, n_ctx, n_seg, scale, ft=512):
    w = p.shape[1]
    kern = functools.partial(_ctx_dft_kernel, scale=scale)
    spec = pl.BlockSpec((n_ctx, ft), lambda b, j: (row_blk0 + b, j))
    return pl.pallas_call(
        kern,
        grid=(n_seg, w // ft),
        in_specs=[spec, spec, pl.BlockSpec((n_ctx, 2 * n_ctx), lambda b, j: (0, 0))],
        out_specs=pl.BlockSpec((n_ctx, ft), lambda b, j: (b, j)),
        out_shape=jax.ShapeDtypeStruct((n_seg * n_ctx, w), BF16),
        compiler_params=_cparams(2),
        name="ctx_dft",
    )(p, q, mc)


def _dft_tables(n_seq, n_ctx, c):
    def cs(num, den):
        ang = (2.0 * math.pi / den) * (num % den).astype(F32)
        return jnp.cos(ang), jnp.sin(ang)

    ic = jnp.arange(c, dtype=jnp.int32)
    cc, sc = cs(ic[:, None] * ic[None, :], c)
    wc = jnp.concatenate([cc, -sc], axis=1).astype(BF16)

    n2 = DFT_INNER
    n1 = n_seq // n2
    i1 = jnp.arange(n1, dtype=jnp.int32)[:, None, None]
    k2 = jnp.arange(n2, dtype=jnp.int32)[None, :, None]
    j2 = jnp.arange(n2, dtype=jnp.int32)[None, None, :]
    c1, s1 = cs((i1 + n1 * j2) * k2, n_seq)
    m1 = jnp.concatenate([jnp.concatenate([c1, s1], axis=2),
                          jnp.concatenate([-s1, c1], axis=2)], axis=1).astype(BF16)
    k1 = jnp.arange(n1, dtype=jnp.int32)
    c2, s2 = cs(k1[:, None] * k1[None, :], n1)
    m2 = jnp.concatenate([c2, s2], axis=1).astype(BF16)

    il = jnp.arange(n_ctx, dtype=jnp.int32)
    cl, sl = cs(il[:, None] * il[None, :], n_ctx)
    mc = jnp.concatenate([cl, sl], axis=1).astype(BF16)
    return wc, m1, m2, mc


def _exact_tri_matmul(tri, v):
    h1 = v.astype(BF16)
    r1 = v - h1.astype(F32)
    h2 = r1.astype(BF16)
    h3 = (r1 - h2.astype(F32)).astype(BF16)
    return (jnp.dot(tri, h1, preferred_element_type=F32)
            + jnp.dot(tri, h2, preferred_element_type=F32)
            + jnp.dot(tri, h3, preferred_element_type=F32))


def _gla_kernel(q_ref, k_ref, v_ref, lr_ref, wg_ref, bg_ref, s0_ref, o_ref, sout_ref, st_ref,
                *, rank, q_scale):
    d = pl.program_id(2)
    t = pl.program_id(3)
    rows = q_ref.shape[0]

    @pl.when(t == 0)
    def _():
        st_ref[...] = s0_ref[...]

    ii = lax.broadcasted_iota(jnp.int32, (CHUNK, CHUNK), 0)
    jj = lax.broadcasted_iota(jnp.int32, (CHUNK, CHUNK), 1)

    def run(reverse):
        keep = (jj >= ii) if reverse else (jj <= ii)
        tri = jnp.where(keep, 1.0, 0.0).astype(BF16)
        wg = wg_ref[...].astype(BF16)
        n_chunks = rows // CHUNK
        order = range(n_chunks - 1, -1, -1) if reverse else range(n_chunks)
        c0 = rank if reverse else 0
        for j in order:
            rs = slice(j * CHUNK, (j + 1) * CHUNK)
            lr = lr_ref[rs, c0:c0 + rank].astype(BF16)
            gl = jnp.dot(lr, wg, preferred_element_type=F32) + bg_ref[...]
            logg = (jnp.minimum(gl, 0.0) - jnp.log1p(jnp.exp(-jnp.abs(gl)))) * (1.0 / GATE_TAU)
            cum = _exact_tri_matmul(tri, logg)
            tot = cum[0:1, :] if reverse else cum[CHUNK - 1:CHUNK, :]
            q = q_ref[rs, :] * q_scale
            k = k_ref[rs, :]
            v = v_ref[rs, :]
            qd = (q * jnp.exp(cum)).astype(BF16)
            ki = (k * jnp.exp(-cum)).astype(BF16)
            kt = (k * jnp.exp(tot - cum)).astype(BF16)
            att = lax.dot_general(qd, ki, (((1,), (1,)), ((), ())), preferred_element_type=F32)
            att = jnp.where(keep, att, 0.0).astype(BF16)
            st = st_ref[...]
            o = jnp.dot(att, v.astype(BF16), preferred_element_type=F32)
            o = o + lax.dot_general(qd, st.astype(BF16), (((1,), (1,)), ((), ())),
                                    preferred_element_type=F32)
            o_ref[rs, :] = o
            upd = jnp.dot(v.T.astype(BF16), kt, preferred_element_type=F32)
            st_ref[...] = st * jnp.exp(tot) + upd

    @pl.when(d == 0)
    def _():
        run(False)

    @pl.when(d == 1)
    def _():
        run(True)

    @pl.when(t == pl.num_programs(3) - 1)
    def _():
        sout_ref[...] = st_ref[...]


def _gla(z, lr, w_gate2, b_gate, s0, row0, seg_len, n_seg, q_col, k_col, v_col, dk, dv):
    heads = GLA_HEADS
    rank = w_gate2.shape[1]
    rows = GLA_ROWS
    nt = seg_len // rows
    blk0 = row0 // rows

    def oblk(b, h, d, s):
        return b * nt + s + d * (nt - 1 - 2 * s)

    def rblk(b, h, d, s):
        return blk0 + oblk(b, h, d, s)

    kern = functools.partial(_gla_kernel, rank=rank, q_scale=dk ** -0.5)
    return pl.pallas_call(
        kern,
        grid=(n_seg, heads, N_DIR, nt),
        in_specs=[pl.BlockSpec((rows, dk), lambda b, h, d, s: (rblk(b, h, d, s), q_col // dk + h)),
                  pl.BlockSpec((rows, dk), lambda b, h, d, s: (rblk(b, h, d, s), k_col // dk + h)),
                  pl.BlockSpec((rows, dv), lambda b, h, d, s: (rblk(b, h, d, s), v_col // dv + h)),
                  pl.BlockSpec((rows, lr.shape[1]), lambda b, h, d, s: (rblk(b, h, d, s), 0)),
                  pl.BlockSpec((None, rank, dk), lambda b, h, d, s: (d, 0, h)),
                  pl.BlockSpec((None, 1, dk), lambda b, h, d, s: (d, 0, h)),
                  pl.BlockSpec((None, None, None, dv, dk), lambda b, h, d, s: (b, d, h, 0, 0))],
        out_specs=[pl.BlockSpec((None, rows, dv), lambda b, h, d, s: (d, oblk(b, h, d, s), h)),
                   pl.BlockSpec((None, None, None, dv, dk), lambda b, h, d, s: (b, d, h, 0, 0))],
        out_shape=[jax.ShapeDtypeStruct((N_DIR, n_seg * seg_len, heads * dv), F32),
                   jax.ShapeDtypeStruct((n_seg, N_DIR, heads, dv, dk), F32)],
        scratch_shapes=[pltpu.VMEM((dv, dk), F32)],
        compiler_params=_cparams(4),
        name="gla_scan",
    )(z, z, z, lr, w_gate2, b_gate.reshape(N_DIR, 1, -1), s0)


def _gla_post_kernel(of_ref, ob_ref, r_ref, g_ref, o_ref, *, dv):
    o = of_ref[...] + ob_ref[...]
    r = r_ref[...]
    gate = g_ref[...] * (r * _sigmoid(r))
    for h in range(o.shape[1] // dv):
        cs = slice(h * dv, (h + 1) * dv)
        oh = o[:, cs]
        oh = oh * lax.rsqrt(jnp.mean(oh * oh, axis=-1, keepdims=True) + EPS)
        o_ref[:, cs] = (oh * gate[:, cs]).astype(o_ref.dtype)


def _gla_post(o2, z, row0, g_head, r_col, dv, tm=256):
    _, t, gv = o2.shape
    blk0 = row0 // tm
    kern = functools.partial(_gla_post_kernel, dv=dv)
    return pl.pallas_call(
        kern,
        grid=(t // tm,),
        in_specs=[pl.BlockSpec((None, tm, gv), lambda i: (0, i, 0)),
                  pl.BlockSpec((None, tm, gv), lambda i: (1, i, 0)),
                  pl.BlockSpec((tm, gv), lambda i: (blk0 + i, r_col // gv)),
                  pl.BlockSpec((1, gv), lambda i: (0, 0))],
        out_specs=pl.BlockSpec((tm, gv), lambda i: (i, 0)),
        out_shape=jax.ShapeDtypeStruct((t, gv), BF16),
        compiler_params=_cparams(1),
        name="gla_post",
    )(o2, o2, z, g_head)


def _route_kernel(aff_ref, idx_ref, gate_ref, c_ref, m_ref, row_ref, *, cap):
    e = pl.program_id(1)
    n_exp, n = aff_ref.shape
    lanes = 128
    nblk = n // lanes

    @pl.when(e == 0)
    def _():
        a = aff_ref[...]
        bits = pltpu.bitcast(a, jnp.int32)
        thr = jnp.zeros((n_exp, 1), jnp.int32)
        def count(pred):
            return jnp.sum(jnp.where(pred, 1.0, 0.0), axis=1, keepdims=True)

        for b in range(30, -1, -1):
            cand = thr | (1 << b)
            thr = jnp.where(count(bits >= cand) >= cap, cand, thr)
        gt = bits > thr
        eq = bits == thr
        need = cap - count(gt)
        pos = lax.broadcasted_iota(jnp.int32, (n_exp, n), 1)
        bound = jnp.zeros((n_exp, 1), jnp.int32)
        for b in range(int(math.log2(n)), -1, -1):
            cand = bound + (1 << b)
            bound = jnp.where(count(eq & (pos < cand)) <= need, cand, bound)
        sel = gt | (eq & (pos < bound))
        m = jnp.where(sel, 1.0, 0.0)
        m_ref[...] = m
        ri = lax.broadcasted_iota(jnp.int32, (lanes, lanes), 0)
        ci = lax.broadcasted_iota(jnp.int32, (lanes, lanes), 1)
        tri = jnp.where(ri <= ci, 1.0, 0.0).astype(BF16)
        off = jnp.zeros((n_exp, 1), F32)
        for j in range(nblk):
            cj = jnp.dot(m[:, j * lanes:(j + 1) * lanes].astype(BF16), tri,
                         preferred_element_type=F32) + off
            c_ref[:, j * lanes:(j + 1) * lanes] = cj
            off = cj[:, lanes - 1:lanes]

    pick = lax.broadcasted_iota(jnp.int32, (n_exp, n), 0) == e
    for k, src in enumerate((c_ref, m_ref, aff_ref)):
        row_ref[k:k + 1, :] = jnp.sum(jnp.where(pick, src[...], 0.0), axis=0, keepdims=True)

    rows = min(cap, 128)
    for rb in range(cap // rows):
        slot1 = (lax.broadcasted_iota(jnp.int32, (rows, lanes), 0) + (rb * rows + 1)).astype(F32)

        def body(j, carry):
            acc_i, acc_g = carry
            col = pl.multiple_of(j * lanes, lanes)
            cj = row_ref[0:1, pl.ds(col, lanes)]
            mj = row_ref[1:2, pl.ds(col, lanes)]
            aj = row_ref[2:3, pl.ds(col, lanes)]
            acc_i = acc_i + jnp.where(cj < slot1, 1.0, 0.0)
            acc_g = acc_g + jnp.where((cj == slot1) & (mj > 0.5), aj, 0.0)
            return acc_i, acc_g

        acc_i, acc_g = lax.fori_loop(
            0, nblk, body, (jnp.zeros((rows, lanes), F32), jnp.zeros((rows, lanes), F32)))
        idx_ref[rb * rows:(rb + 1) * rows, :] = jnp.sum(acc_i, axis=1, keepdims=True).astype(jnp.int32)
        gate_ref[rb * rows:(rb + 1) * rows, :] = jnp.sum(acc_g, axis=1, keepdims=True)


def _route(aff_t, col_blk0, n_set, n_sets, cap):
    n_exp = aff_t.shape[0]
    kern = functools.partial(_route_kernel, cap=cap)
    return pl.pallas_call(
        kern,
        grid=(n_sets, n_exp),
        in_specs=[pl.BlockSpec((n_exp, n_set), lambda s, e: (0, col_blk0 + s))],
        out_specs=[pl.BlockSpec((None, None, cap, 1), lambda s, e: (s, e, 0, 0)),
                   pl.BlockSpec((None, None, cap, 1), lambda s, e: (s, e, 0, 0))],
        out_shape=[jax.ShapeDtypeStruct((n_sets, n_exp, cap, 1), jnp.int32),
                   jax.ShapeDtypeStruct((n_sets, n_exp, cap, 1), F32)],
        scratch_shapes=[pltpu.VMEM((n_exp, n_set), F32), pltpu.VMEM((n_exp, n_set), F32),
                        pltpu.VMEM((8, n_set), F32)],
        compiler_params=_cparams(2),
        name="route",
    )(aff_t)


def _row_copy(src, src_row, dst, dst_row, sem):
    return pltpu.make_async_copy(src.at[pl.ds(src_row, 1), :], dst.at[pl.ds(dst_row, 1), :], sem)


def _gather_kernel(idx_ref, h_hbm, o_ref, buf, sem):
    n = buf.shape[0]
    base = pl.program_id(0) * n

    def start(r, c):
        _row_copy(h_hbm, idx_ref[base + r], buf, r, sem).start()
        return c

    def wait(r, c):
        _row_copy(h_hbm, idx_ref[base + r], buf, r, sem).wait()
        return c

    lax.fori_loop(0, n, start, 0)
    lax.fori_loop(0, n, wait, 0)
    o_ref[...] = buf[...].astype(o_ref.dtype)


def _gather_rows(idx, h, n=DMA_ROWS):
    r = idx.shape[0]
    n = min(n, r)
    d = h.shape[1]
    return pl.pallas_call(
        _gather_kernel,
        grid_spec=pltpu.PrefetchScalarGridSpec(
            num_scalar_prefetch=1,
            grid=(r // n,),
            in_specs=[pl.BlockSpec(memory_space=pl.ANY)],
            out_specs=pl.BlockSpec((n, d), lambda g, idx_ref: (g, 0)),
            scratch_shapes=[pltpu.VMEM((n, d), F32), pltpu.SemaphoreType.DMA(())]),
        out_shape=jax.ShapeDtypeStruct((r, d), BF16),
        compiler_params=_cparams(1),
        name="gather_rows",
    )(idx, h)


def _scatter_kernel(idx_ref, y_ref, gt_ref, x_in, x_hbm, buf, sem_in, sem_out):
    del x_in
    n = buf.shape[0]
    base = pl.program_id(0) * n

    def start_in(r, c):
        _row_copy(x_hbm, idx_ref[base + r], buf, r, sem_in).start()
        return c

    def wait_in(r, c):
        _row_copy(x_hbm, idx_ref[base + r], buf, r, sem_in).wait()
        return c

    def start_out(r, c):
        _row_copy(buf, r, x_hbm, idx_ref[base + r], sem_out).start()
        return c

    def wait_out(r, c):
        _row_copy(buf, r, x_hbm, idx_ref[base + r], sem_out).wait()
        return c

    lax.fori_loop(0, n, start_in, 0)
    lax.fori_loop(0, n, wait_in, 0)
    buf[...] = buf[...] + gt_ref[...] * y_ref[...]
    lax.fori_loop(0, n, start_out, 0)
    lax.fori_loop(0, n, wait_out, 0)


def _scatter_add_rows(idx, y, x, mod, k_gate, rows_per_set, n_sets, mod_row, n=DMA_ROWS):
    r, d = y.shape
    n = min(n, rows_per_set)
    steps_per_set = rows_per_set // n

    def gate_blk(g, idx_ref):
        row = (g // steps_per_set) % n_sets if mod_row is None else mod_row
        return (row * N_MOD + k_gate, 0, 0)

    return pl.pallas_call(
        _scatter_kernel,
        grid_spec=pltpu.PrefetchScalarGridSpec(
            num_scalar_prefetch=1,
            grid=(r // n,),
            in_specs=[pl.BlockSpec((n, d), lambda g, idx_ref: (g, 0)),
                      pl.BlockSpec((None, 1, d), gate_blk),
                      pl.BlockSpec(memory_space=pl.ANY)],
            out_specs=pl.BlockSpec(memory_space=pl.ANY),
            scratch_shapes=[pltpu.VMEM((n, d), F32), pltpu.SemaphoreType.DMA(()),
                            pltpu.SemaphoreType.DMA(())]),
        out_shape=jax.ShapeDtypeStruct(x.shape, F32),
        input_output_aliases={3: 0},
        compiler_params=_cparams(1),
        name="scatter_add_rows",
    )(idx, y, mod, x)


def _expert_up_kernel(x_ref, wg_ref, wu_ref, o_ref, wgs_ref, wus_ref):
    first = pl.program_id(2) == 0
    _cache_weight(wg_ref, wgs_ref, first)
    _cache_weight(wu_ref, wus_ref, first)
    x = x_ref[...]
    g = jnp.dot(x, wgs_ref[...], preferred_element_type=F32)
    u = jnp.dot(x, wus_ref[...], preferred_element_type=F32)
    o_ref[...] = (g * _sigmoid(g) * u).astype(o_ref.dtype)


def _expert_up(xin, w_g, w_u, l, tr=512, tf=256):
    n_exp, r, d = xin.shape
    ff = w_g.shape[-1]
    tr = min(tr, r)
    wspec = pl.BlockSpec((None, None, d, tf), lambda e, f, i: (l, e, 0, f))
    return pl.pallas_call(
        _expert_up_kernel,
        grid=(n_exp, ff // tf, r // tr),
        in_specs=[pl.BlockSpec((None, tr, d), lambda e, f, i: (e, i, 0)), wspec, wspec],
        out_specs=pl.BlockSpec((None, tr, tf), lambda e, f, i: (e, i, f)),
        out_shape=jax.ShapeDtypeStruct((n_exp, r, ff), BF16),
        scratch_shapes=[pltpu.VMEM((d, tf), BF16), pltpu.VMEM((d, tf), BF16)],
        compiler_params=_cparams(3),
        name="expert_up",
    )(xin, w_g, w_u)


def _expert_down_kernel(h_ref, wd_ref, gate_ref, o_ref, wds_ref):
    _cache_weight(wd_ref, wds_ref, pl.program_id(2) == 0)
    o_ref[...] = jnp.dot(h_ref[...], wds_ref[...], preferred_element_type=F32) * gate_ref[...]


def _expert_down(hid, w_d, l, gates, tr=512, tn=1024):
    n_exp, r, ff = hid.shape
    d = w_d.shape[-1]
    tr = min(tr, r)
    return pl.pallas_call(
        _expert_down_kernel,
        grid=(n_exp, d // tn, r // tr),
        in_specs=[pl.BlockSpec((None, tr, ff), lambda e, j, i: (e, i, 0)),
                  pl.BlockSpec((None, None, ff, tn), lambda e, j, i: (l, e, 0, j)),
                  pl.BlockSpec((None, tr, 1), lambda e, j, i: (e, i, 0))],
        out_specs=pl.BlockSpec((None, tr, tn), lambda e, j, i: (e, i, j)),
        out_shape=jax.ShapeDtypeStruct((n_exp, r, d), F32),
        scratch_shapes=[pltpu.VMEM((ff, tn), BF16)],
        compiler_params=_cparams(3),
        name="expert_down",
    )(hid, w_d, gates)


def _moe(x, h2, aff_t, mod, k_gate, w_g, w_u, w_d, l, row0, n_set, n_sets, mod_row):
    n_exp = aff_t.shape[0]
    d = x.shape[1]
    cap = CAPACITY_FACTOR * n_set // n_exp
    idx, gates = _route(aff_t, row0 // n_set, n_set, n_sets, cap)
    offs = row0 + n_set * jnp.arange(n_sets, dtype=jnp.int32)
    rows = (idx.reshape(n_sets, n_exp, cap) + offs[:, None, None]).transpose(1, 0, 2).reshape(-1)
    gates = gates.reshape(n_sets, n_exp, cap).transpose(1, 0, 2).reshape(n_exp, n_sets * cap, 1)
    xin = _gather_rows(rows, h2).reshape(n_exp, n_sets * cap, d)
    hid = _expert_up(xin, w_g, w_u, l)
    y = _expert_down(hid, w_d, l, gates).reshape(n_exp * n_sets * cap, d)
    return _scatter_add_rows(rows, y, x, mod, k_gate, cap, n_sets, mod_row)


def kernel(x, c, ctx, c_ctx, w_ada, b_ada, g_norm1, w_in, w_gate2, b_gate, g_head, w_branch_a,
           w_branch_b, w_out, g_norm2, w_router, w_exp_gate, w_exp_up, w_exp_down, g_final):
    bsz, n_lat, d = x.shape
    n_ctx = ctx.shape[1]
    depth = w_ada.shape[0]
    fw = w_branch_a.shape[1]
    gk = w_gate2.shape[3]
    gv = g_head.shape[1]
    rank = w_gate2.shape[2]
    dk, dv = gk // GLA_HEADS, gv // GLA_HEADS
    cgrp = fw // N_FOURIER_GROUPS
    lat_rows = bsz * n_lat
    q_col, k_col, v_col, r_col = fw, fw + gk, fw + 2 * gk, fw + 2 * gk + gv
    lr_col = r_col + gv
    gl_col = lr_col + N_DIR * rank
    lr_w = 128

    xs = jnp.concatenate([x.reshape(lat_rows, d), ctx.reshape(bsz * n_ctx, d)], axis=0)
    c_rows = jnp.concatenate([c, c_ctx[None, :], jnp.zeros((8 - bsz - 1, d), F32)], axis=0)
    mod_all = _adaln(c_rows, w_ada, b_ada)
    wc, m1, m2, mc = _dft_tables(n_lat, n_ctx, cgrp)
    zero_state = jnp.zeros((bsz, N_DIR, GLA_HEADS, dv, dk), F32)

    for l in range(depth):
        last = l == depth - 1
        mod = mod_all[l].reshape(8 * N_MOD, 1, d)
        h = _norm_mod(xs, g_norm1[l][None, :], mod, 1, 0, n_lat, bsz, BF16)
        z = _matmul(h, w_in, l, 0, lr_col)
        lr = _matmul(h, w_in, l, lr_col, lr_w)
        gl = _matmul(h, w_in[l, :, gl_col:].astype(BF16), 0, 0, 2 * d)
        p, q = _chan_dft(z, wc, fw)
        four_l = _seq_dft(p, q, m1, m2, n_lat, bsz, 1.0 / math.sqrt(n_lat * cgrp))
        four_c = _ctx_dft(p, q, mc, lat_rows // n_ctx, n_ctx, bsz, 1.0 / math.sqrt(n_ctx * cgrp))
        o_c, s_c = _gla(z, lr, w_gate2[l], b_gate[l], zero_state, lat_rows, n_ctx, bsz,
                        q_col, k_col, v_col, dk, dv)
        o_l, _ = _gla(z, lr, w_gate2[l], b_gate[l], s_c, 0, n_lat, bsz,
                      q_col, k_col, v_col, dk, dv)
        ogla_l = _gla_post(o_l, z, 0, g_head[l][None, :], r_col, dv)
        ogla_c = _gla_post(o_c, z, lat_rows, g_head[l][None, :], r_col, dv)
        m = _merge(four_l, four_c, ogla_l, ogla_c, w_branch_a, w_branch_b, l, gl)
        xs = _out_residual(m, w_out, l, xs, mod, 2, n_lat, bsz)
        wr_t = w_router[l].T.astype(BF16)
        h2, aff_t = _norm_mod_router(xs, g_norm2[l][None, :], mod, 4, 3, wr_t, n_lat, bsz)
        xs = _moe(xs, h2, aff_t, mod, 5, w_exp_gate, w_exp_up, w_exp_down, l,
                  0, n_lat, bsz, None)
        if not last:
            xs = _moe(xs, h2, aff_t, mod, 5, w_exp_gate, w_exp_up, w_exp_down, l,
                      lat_rows, n_ctx, bsz, bsz)
    return _final_norm(xs, g_final[None, :], lat_rows).reshape(bsz, n_lat, d)
```

```python
import functools
import math

import jax
import jax.numpy as jnp
from jax import lax
from jax.experimental import pallas as pl
from jax.experimental.pallas import tpu as pltpu

F32 = jnp.float32
BF16 = jnp.bfloat16

N_FOURIER_GROUPS = 4
GLA_HEADS = 4
GATE_TAU = 16.0
CHUNK = 64
N_DIR = 2
CAPACITY_FACTOR = 2
N_MOD = 6
EPS = 1e-6

DFT_INNER = 128
DFT_UNROLL = 8
VMEM_LIMIT = 56 << 20
GLA_ROWS = 256
DMA_ROWS = 128
DMA_UNROLL = 8


def _cparams(n_axes):
    return pltpu.CompilerParams(dimension_semantics=("arbitrary",) * n_axes,
                                vmem_limit_bytes=VMEM_LIMIT)


def _sigmoid(v):
    return 1.0 / (1.0 + jnp.exp(-v))


def _ada_kernel(c_ref, w_ref, b_ref, o_ref):
    c = c_ref[...]
    a = (c * _sigmoid(c)).astype(BF16)
    o_ref[...] = jnp.dot(a, w_ref[...].astype(BF16), preferred_element_type=F32) + b_ref[...]


def _adaln(c_rows, w_ada, b_ada, tn=512):
    depth, d, n = w_ada.shape
    r = c_rows.shape[0]
    return pl.pallas_call(
        _ada_kernel,
        grid=(depth, n // tn),
        in_specs=[pl.BlockSpec((r, d), lambda l, j: (0, 0)),
                  pl.BlockSpec((None, d, tn), lambda l, j: (l, 0, j)),
                  pl.BlockSpec((None, 1, tn), lambda l, j: (l, 0, j))],
        out_specs=pl.BlockSpec((None, r, tn), lambda l, j: (l, 0, j)),
        out_shape=jax.ShapeDtypeStruct((depth, r, n), F32),
        compiler_params=_cparams(2),
        name="adaln",
    )(c_rows, w_ada, b_ada.reshape(depth, 1, n))


def _normed(x_ref, g_ref, sc_ref, sh_ref):
    x = x_ref[...]
    y = x * lax.rsqrt(jnp.mean(x * x, axis=-1, keepdims=True) + EPS) * g_ref[...]
    return y * (1.0 + sc_ref[...]) + sh_ref[...]


def _norm_kernel(x_ref, g_ref, sc_ref, sh_ref, o_ref):
    o_ref[...] = _normed(x_ref, g_ref, sc_ref, sh_ref).astype(o_ref.dtype)


def _norm_router_kernel(x_ref, g_ref, sc_ref, sh_ref, wr_ref, o_ref, aff_ref):
    h = _normed(x_ref, g_ref, sc_ref, sh_ref)
    o_ref[...] = h
    logits = lax.dot_general(wr_ref[...], h.astype(BF16), (((1,), (1,)), ((), ())),
                             preferred_element_type=F32)
    e = jnp.exp(logits - jnp.max(logits, axis=0, keepdims=True))
    aff_ref[...] = e / jnp.sum(e, axis=0, keepdims=True)


def _seg_of(i, tm, n_lat, n_seg):
    return jnp.minimum((i * tm) // n_lat, n_seg)


def _norm_mod(x, g, mod, k_scale, k_shift, n_lat, n_seg, out_dtype, tm=256):
    t, d = x.shape
    seg = functools.partial(_seg_of, tm=tm, n_lat=n_lat, n_seg=n_seg)
    return pl.pallas_call(
        _norm_kernel,
        grid=(t // tm,),
        in_specs=[pl.BlockSpec((tm, d), lambda i: (i, 0)),
                  pl.BlockSpec((1, d), lambda i: (0, 0)),
                  pl.BlockSpec((None, 1, d), lambda i: (seg(i) * N_MOD + k_scale, 0, 0)),
                  pl.BlockSpec((None, 1, d), lambda i: (seg(i) * N_MOD + k_shift, 0, 0))],
        out_specs=pl.BlockSpec((tm, d), lambda i: (i, 0)),
        out_shape=jax.ShapeDtypeStruct((t, d), out_dtype),
        compiler_params=_cparams(1),
        name="norm_mod",
    )(x, g, mod, mod)


def _norm_mod_router(x, g, mod, k_scale, k_shift, wr_t, n_lat, n_seg, tm=256):
    t, d = x.shape
    e = wr_t.shape[0]
    seg = functools.partial(_seg_of, tm=tm, n_lat=n_lat, n_seg=n_seg)
    return pl.pallas_call(
        _norm_router_kernel,
        grid=(t // tm,),
        in_specs=[pl.BlockSpec((tm, d), lambda i: (i, 0)),
                  pl.BlockSpec((1, d), lambda i: (0, 0)),
                  pl.BlockSpec((None, 1, d), lambda i: (seg(i) * N_MOD + k_scale, 0, 0)),
                  pl.BlockSpec((None, 1, d), lambda i: (seg(i) * N_MOD + k_shift, 0, 0)),
                  pl.BlockSpec((e, d), lambda i: (0, 0))],
        out_specs=[pl.BlockSpec((tm, d), lambda i: (i, 0)),
                   pl.BlockSpec((e, tm), lambda i: (0, i))],
        out_shape=[jax.ShapeDtypeStruct((t, d), F32),
                   jax.ShapeDtypeStruct((e, t), F32)],
        compiler_params=_cparams(1),
        name="norm_mod_router",
    )(x, g, mod, mod, wr_t)


def _final_norm_kernel(x_ref, g_ref, o_ref):
    x = x_ref[...]
    o_ref[...] = x * lax.rsqrt(jnp.mean(x * x, axis=-1, keepdims=True) + EPS) * g_ref[...]


def _final_norm(x, g, rows, tm=256):
    d = x.shape[1]
    return pl.pallas_call(
        _final_norm_kernel,
        grid=(rows // tm,),
        in_specs=[pl.BlockSpec((tm, d), lambda i: (i, 0)),
                  pl.BlockSpec((1, d), lambda i: (0, 0))],
        out_specs=pl.BlockSpec((tm, d), lambda i: (i, 0)),
        out_shape=jax.ShapeDtypeStruct((rows, d), F32),
        compiler_params=_cparams(1),
        name="final_norm",
    )(x, g)


def _cache_weight(w_ref, ws_ref, first):
    @pl.when(first)
    def _():
        ws_ref[...] = w_ref[...].astype(BF16)


def _mm_t_kernel(a_ref, bt_ref, o_ref, bs_ref):
    @pl.when(pl.program_id(1) == 0)
    def _():
        bs_ref[...] = bt_ref[0].T.astype(BF16)

    o_ref[...] = jnp.dot(a_ref[...], bs_ref[...], preferred_element_type=F32).astype(o_ref.dtype)


def _layer_spec(block, index_map, w, l):
    if w.ndim == len(block):
        return pl.BlockSpec(block, index_map)
    return pl.BlockSpec((None,) + block, lambda *g: (l,) + index_map(*g))


def _matmul_t(a, bt, l, col0, ncols, out_dtype=F32, tm=512, tn=512):
    t, k = a.shape
    tn = min(tn, ncols)
    assert col0 % 8 == 0 and t % tm == 0 and ncols % tn == 0
    return pl.pallas_call(
        _mm_t_kernel,
        grid=(ncols // tn, t // tm),
        in_specs=[pl.BlockSpec((tm, k), lambda j, i: (i, 0)),
                  pl.BlockSpec((pl.Element(1), pl.Element(tn), pl.Element(k)),
                               lambda j, i: (l, pl.multiple_of(col0 + j * tn, 8), 0))],
        out_specs=pl.BlockSpec((tm, tn), lambda j, i: (i, j)),
        out_shape=jax.ShapeDtypeStruct((t, ncols), out_dtype),
        scratch_shapes=[pltpu.VMEM((k, tn), BF16)],
        compiler_params=_cparams(2),
        name="matmul_t",
    )(a, bt)


def _merge_kernel(fl_ref, fc_ref, ol_ref, oc_ref, wa_ref, wb_ref, ga_ref, gb_ref, o_ref,
                  was_ref, wbs_ref, *, lat_tiles):
    i = pl.program_id(1)
    _cache_weight(wa_ref, was_ref, i == 0)
    _cache_weight(wb_ref, wbs_ref, i == 0)

    def compute(fa_ref, oa_ref):
        ya = jnp.dot(fa_ref[...], was_ref[...], preferred_element_type=F32)
        yb = jnp.dot(oa_ref[...], wbs_ref[...], preferred_element_type=F32)
        o_ref[...] = (_sigmoid(ga_ref[...]) * ya + _sigmoid(gb_ref[...]) * yb).astype(o_ref.dtype)

    @pl.when(i < lat_tiles)
    def _():
        compute(fl_ref, ol_ref)

    @pl.when(i >= lat_tiles)
    def _():
        compute(fc_ref, oc_ref)


def _merge(four_lat, four_ctx, ogla_lat, ogla_ctx, w_a, w_b, l, gates, tm=512, tn=512):
    ka, kb = four_lat.shape[1], ogla_lat.shape[1]
    lat_tiles = four_lat.shape[0] // tm
    t = gates.shape[0]
    d = w_a.shape[-1]
    nj = d // tn

    def lat(j, i):
        return (jnp.minimum(i, lat_tiles - 1), 0)

    def ctx(j, i):
        return (jnp.maximum(i - lat_tiles, 0), 0)

    return pl.pallas_call(
        functools.partial(_merge_kernel, lat_tiles=lat_tiles),
        grid=(nj, t // tm),
        in_specs=[pl.BlockSpec((tm, ka), lat),
                  pl.BlockSpec((tm, ka), ctx),
                  pl.BlockSpec((tm, kb), lat),
                  pl.BlockSpec((tm, kb), ctx),
                  _layer_spec((ka, tn), lambda j, i: (0, j), w_a, l),
                  _layer_spec((kb, tn), lambda j, i: (0, j), w_b, l),
                  pl.BlockSpec((tm, tn), lambda j, i: (i, j)),
                  pl.BlockSpec((tm, tn), lambda j, i: (i, nj + j))],
        out_specs=pl.BlockSpec((tm, tn), lambda j, i: (i, j)),
        out_shape=jax.ShapeDtypeStruct((t, d), BF16),
        scratch_shapes=[pltpu.VMEM((ka, tn), BF16), pltpu.VMEM((kb, tn), BF16)],
        compiler_params=_cparams(2),
        name="merge",
    )(four_lat, four_ctx, ogla_lat, ogla_ctx, w_a, w_b, gates, gates)


def _out_res_kernel(a_ref, b_ref, x_ref, gt_ref, o_ref, bs_ref):
    _cache_weight(b_ref, bs_ref, pl.program_id(1) == 0)
    y = jnp.dot(a_ref[...], bs_ref[...], preferred_element_type=F32)
    o_ref[...] = x_ref[...] + gt_ref[...] * y


def _out_residual(m, w_out, l, x, mod, k_gate, n_lat, n_seg, tm=512, tn=512):
    t, k = m.shape
    d = w_out.shape[-1]
    seg = functools.partial(_seg_of, tm=tm, n_lat=n_lat, n_seg=n_seg)
    return pl.pallas_call(
        _out_res_kernel,
        grid=(d // tn, t // tm),
        in_specs=[pl.BlockSpec((tm, k), lambda j, i: (i, 0)),
                  _layer_spec((k, tn), lambda j, i: (0, j), w_out, l),
                  pl.BlockSpec((tm, tn), lambda j, i: (i, j)),
                  pl.BlockSpec((None, 1, tn), lambda j, i: (seg(i) * N_MOD + k_gate, 0, j))],
        out_specs=pl.BlockSpec((tm, tn), lambda j, i: (i, j)),
        out_shape=jax.ShapeDtypeStruct((t, d), F32),
        scratch_shapes=[pltpu.VMEM((k, tn), BF16)],
        compiler_params=_cparams(2),
        name="out_residual",
    )(m, w_out, x, mod)


def _chan_dft_kernel(u_ref, w_ref, p_ref, q_ref):
    c = p_ref.shape[1]
    y = jnp.dot(u_ref[...].astype(BF16), w_ref[...], preferred_element_type=F32)
    p_ref[...] = y[:, :c]
    q_ref[...] = y[:, c:]


def _chan_dft(z, wc, width, tm=512):
    t = z.shape[0]
    c = wc.shape[0]
    return pl.pallas_call(
        _chan_dft_kernel,
        grid=(t // tm, width // c),
        in_specs=[pl.BlockSpec((tm, c), lambda i, g: (i, g)),
                  pl.BlockSpec((c, 2 * c), lambda i, g: (0, 0))],
        out_specs=[pl.BlockSpec((tm, c), lambda i, g: (i, g)),
                   pl.BlockSpec((tm, c), lambda i, g: (i, g))],
        out_shape=[jax.ShapeDtypeStruct((t, width), F32)] * 2,
        compiler_params=_cparams(2),
        name="chan_dft",
    )(z, wc)


def _seq_dft_kernel(p_ref, q_ref, m1_ref, m2_ref, o_ref, bs_ref, os_ref, *, n1, scale):
    n2 = DFT_INNER

    def stage1(i, carry):
        xp = p_ref[pl.ds(i, n2, stride=n1), :]
        xq = q_ref[pl.ds(i, n2, stride=n1), :]
        xs = jnp.concatenate([xp, xq], axis=0).astype(BF16)
        row0 = pl.multiple_of(i * (2 * n2), 2 * n2)
        bs_ref[pl.ds(row0, 2 * n2), :] = jnp.dot(m1_ref[i], xs, preferred_element_type=F32)
        return carry

    lax.fori_loop(0, n1, stage1, 0, unroll=DFT_UNROLL)

    def stage2(k2, carry):
        are = bs_ref[pl.ds(k2, n1, stride=2 * n2), :]
        aim = bs_ref[pl.ds(n2 + k2, n1, stride=2 * n2), :]
        a = jnp.concatenate([are, aim], axis=0).astype(BF16)
        y = jnp.dot(m2_ref[...], a, preferred_element_type=F32) * scale
        os_ref[pl.ds(k2, n1, stride=n2), :] = y
        return carry

    lax.fori_loop(0, n2, stage2, 0, unroll=DFT_UNROLL)
    o_ref[...] = os_ref[...].astype(o_ref.dtype)


def _seq_dft(p, q, m1, m2, n_seq, n_seg, scale, ft=128):
    t, w = p.shape
    n1 = n_seq // DFT_INNER
    kern = functools.partial(_seq_dft_kernel, n1=n1, scale=scale)
    return pl.pallas_call(
        kern,
        grid=(n_seg, w // ft),
        in_specs=[pl.BlockSpec((n_seq, ft), lambda b, j: (b, j)),
                  pl.BlockSpec((n_seq, ft), lambda b, j: (b, j)),
                  pl.BlockSpec(memory_space=pltpu.VMEM),
                  pl.BlockSpec(memory_space=pltpu.VMEM)],
        out_specs=pl.BlockSpec((n_seq, ft), lambda b, j: (b, j)),
        out_shape=jax.ShapeDtypeStruct((n_seg * n_seq, w), BF16),
        scratch_shapes=[pltpu.VMEM((n1 * 2 * DFT_INNER, ft), F32), pltpu.VMEM((n_seq, ft), F32)],
        compiler_params=_cparams(2),
        name="seq_dft",
    )(p, q, m1, m2)


def _ctx_dft_kernel(p_ref, q_ref, m_ref, o_ref, *, scale):
    a = jnp.concatenate([p_ref[...], q_ref[...]], axis=0).astype(BF16)
    o_ref[...] = (jnp.dot(m_ref[...], a, preferred_element_type=F32) * scale).astype(o_ref.dtype)


def _ctx_dft(p, q, mc, row_blk0, n_ctx, n_seg, scale, ft=512):
    w = p.shape[1]
    kern = functools.partial(_ctx_dft_kernel, scale=scale)
    spec = pl.BlockSpec((n_ctx, ft), lambda b, j: (row_blk0 + b, j))
    return pl.pallas_call(
        kern,
        grid=(n_seg, w // ft),
        in_specs=[spec, spec, pl.BlockSpec((n_ctx, 2 * n_ctx), lambda b, j: (0, 0))],
        out_specs=pl.BlockSpec((n_ctx, ft), lambda b, j: (b, j)),
        out_shape=jax.ShapeDtypeStruct((n_seg * n_ctx, w), BF16),
        compiler_params=_cparams(2),
        name="ctx_dft",
    )(p, q, mc)


def _dft_tables(n_seq, n_ctx, c):
    def cs(num, den):
        ang = (2.0 * math.pi / den) * (num % den).astype(F32)
        return jnp.cos(ang), jnp.sin(ang)

    ic = jnp.arange(c, dtype=jnp.int32)
    cc, sc = cs(ic[:, None] * ic[None, :], c)
    wc = jnp.concatenate([cc, -sc], axis=1).astype(BF16)

    n2 = DFT_INNER
    n1 = n_seq // n2
    i1 = jnp.arange(n1, dtype=jnp.int32)[:, None, None]
    k2 = jnp.arange(n2, dtype=jnp.int32)[None, :, None]
    j2 = jnp.arange(n2, dtype=jnp.int32)[None, None, :]
    c1, s1 = cs((i1 + n1 * j2) * k2, n_seq)
    m1 = jnp.concatenate([jnp.concatenate([c1, s1], axis=2),
                          jnp.concatenate([-s1, c1], axis=2)], axis=1).astype(BF16)
    k1 = jnp.arange(n1, dtype=jnp.int32)
    c2, s2 = cs(k1[:, None] * k1[None, :], n1)
    m2 = jnp.concatenate([c2, s2], axis=1).astype(BF16)

    il = jnp.arange(n_ctx, dtype=jnp.int32)
    cl, sl = cs(il[:, None] * il[None, :], n_ctx)
    mc = jnp.concatenate([cl, sl], axis=1).astype(BF16)
    return wc, m1, m2, mc


def _exact_tri_matmul(tri, v):
    h1 = v.astype(BF16)
    r1 = v - h1.astype(F32)
    h2 = r1.astype(BF16)
    h3 = (r1 - h2.astype(F32)).astype(BF16)
    return (jnp.dot(tri, h1, preferred_element_type=F32)
            + jnp.dot(tri, h2, preferred_element_type=F32)
            + jnp.dot(tri, h3, preferred_element_type=F32))


def _gla_kernel(qf_ref, kf_ref, vf_ref, lrf_ref, qb_ref, kb_ref, vb_ref, lrb_ref, wg_ref, bg_ref,
                s0_ref, of_ref, ob_ref, sout_ref, st_ref, *, rank, q_scale):
    t = pl.program_id(2)
    rows = qf_ref.shape[0]
    n_chunks = rows // CHUNK

    @pl.when(t == 0)
    def _():
        st_ref[...] = s0_ref[...]

    ii = lax.broadcasted_iota(jnp.int32, (CHUNK, CHUNK), 0)
    jj = lax.broadcasted_iota(jnp.int32, (CHUNK, CHUNK), 1)
    dirs = ((qf_ref, kf_ref, vf_ref, lrf_ref, of_ref, jj <= ii),
            (qb_ref, kb_ref, vb_ref, lrb_ref, ob_ref, jj >= ii))

    logg = []
    for d, (_, _, _, lr_ref, _, _) in enumerate(dirs):
        lr = lr_ref[:, d * rank:(d + 1) * rank].astype(BF16)
        gl = jnp.dot(lr, wg_ref[d].astype(BF16), preferred_element_type=F32) + bg_ref[d]
        logg.append((jnp.minimum(gl, 0.0) - jnp.log1p(jnp.exp(-jnp.abs(gl)))) * (1.0 / GATE_TAU))

    def chunk(d, j):
        q_ref, k_ref, v_ref, _, o_ref, keep = dirs[d]
        rs = slice(j * CHUNK, (j + 1) * CHUNK)
        tri = jnp.where(keep, 1.0, 0.0).astype(BF16)
        cum = _exact_tri_matmul(tri, logg[d][rs, :])
        tot = cum[0:1, :] if d else cum[CHUNK - 1:CHUNK, :]
        q = q_ref[rs, :] * q_scale
        k = k_ref[rs, :]
        v = v_ref[rs, :]
        qd = (q * jnp.exp(cum)).astype(BF16)
        ki = (k * jnp.exp(-cum)).astype(BF16)
        kt = (k * jnp.exp(tot - cum)).astype(BF16)
        att = lax.dot_general(qd, ki, (((1,), (1,)), ((), ())), preferred_element_type=F32)
        att = jnp.where(keep, att, 0.0).astype(BF16)
        st = st_ref[d]
        o = jnp.dot(att, v.astype(BF16), preferred_element_type=F32)
        o = o + lax.dot_general(qd, st.astype(BF16), (((1,), (1,)), ((), ())),
                                preferred_element_type=F32)
        o_ref[rs, :] = o
        upd = jnp.dot(v.T.astype(BF16), kt, preferred_element_type=F32)
        st_ref[d] = st * jnp.exp(tot) + upd

    for j in range(n_chunks):
        chunk(0, j)
        chunk(1, n_chunks - 1 - j)

    @pl.when(t == pl.num_programs(2) - 1)
    def _():
        sout_ref[...] = st_ref[...]


def _gla(z, lr, w_gate2, b_gate, s0, row0, seg_len, n_seg, q_col, k_col, v_col, dk, dv):
    heads = GLA_HEADS
    rank = w_gate2.shape[1]
    rows = GLA_ROWS
    nt = seg_len // rows
    blk0 = row0 // rows

    def fwd(b, s):
        return b * nt + s

    def bwd(b, s):
        return b * nt + nt - 1 - s

    def zspec(width, col, blk):
        return pl.BlockSpec((rows, width), lambda b, h, s: (blk0 + blk(b, s), col // width + h))

    def lrspec(blk):
        return pl.BlockSpec((rows, lr.shape[1]), lambda b, h, s: (blk0 + blk(b, s), 0))

    state = pl.BlockSpec((None, None, N_DIR, dv, dk), lambda b, h, s: (b, h, 0, 0, 0))
    kern = functools.partial(_gla_kernel, rank=rank, q_scale=dk ** -0.5)
    return pl.pallas_call(
        kern,
        grid=(n_seg, heads, nt),
        in_specs=[zspec(dk, q_col, fwd), zspec(dk, k_col, fwd), zspec(dv, v_col, fwd), lrspec(fwd),
                  zspec(dk, q_col, bwd), zspec(dk, k_col, bwd), zspec(dv, v_col, bwd), lrspec(bwd),
                  pl.BlockSpec((N_DIR, rank, dk), lambda b, h, s: (0, 0, h)),
                  pl.BlockSpec((N_DIR, 1, dk), lambda b, h, s: (0, 0, h)),
                  state],
        out_specs=[pl.BlockSpec((rows, dv), lambda b, h, s: (fwd(b, s), h)),
                   pl.BlockSpec((rows, dv), lambda b, h, s: (bwd(b, s), h)),
                   state],
        out_shape=[jax.ShapeDtypeStruct((n_seg * seg_len, heads * dv), F32),
                   jax.ShapeDtypeStruct((n_seg * seg_len, heads * dv), F32),
                   jax.ShapeDtypeStruct((n_seg, heads, N_DIR, dv, dk), F32)],
        scratch_shapes=[pltpu.VMEM((N_DIR, dv, dk), F32)],
        compiler_params=_cparams(3),
        name="gla_scan",
    )(z, z, z, lr, z, z, z, lr, w_gate2, b_gate.reshape(N_DIR, 1, -1), s0)


def _gla_post_kernel(of_ref, ob_ref, r_ref, g_ref, o_ref, *, dv):
    o = of_ref[...] + ob_ref[...]
    r = r_ref[...]
    gate = g_ref[...] * (r * _sigmoid(r))
    for h in range(o.shape[1] // dv):
        cs = slice(h * dv, (h + 1) * dv)
        oh = o[:, cs]
        oh = oh * lax.rsqrt(jnp.mean(oh * oh, axis=-1, keepdims=True) + EPS)
        o_ref[:, cs] = (oh * gate[:, cs]).astype(o_ref.dtype)


def _gla_post(o_f, o_b, z, row0, g_head, r_col, dv, tm=256):
    t, gv = o_f.shape
    blk0 = row0 // tm
    kern = functools.partial(_gla_post_kernel, dv=dv)
    return pl.pallas_call(
        kern,
        grid=(t // tm,),
        in_specs=[pl.BlockSpec((tm, gv), lambda i: (i, 0)),
                  pl.BlockSpec((tm, gv), lambda i: (i, 0)),
                  pl.BlockSpec((tm, gv), lambda i: (blk0 + i, r_col // gv)),
                  pl.BlockSpec((1, gv), lambda i: (0, 0))],
        out_specs=pl.BlockSpec((tm, gv), lambda i: (i, 0)),
        out_shape=jax.ShapeDtypeStruct((t, gv), BF16),
        compiler_params=_cparams(1),
        name="gla_post",
    )(o_f, o_b, z, g_head)


def _route_kernel(aff_ref, idx_ref, gate_ref, c_ref, m_ref, row_ref, *, cap):
    e = pl.program_id(1)
    n_exp, n = aff_ref.shape
    lanes = 128
    nblk = n // lanes

    @pl.when(e == 0)
    def _():
        a = aff_ref[...]
        bits = pltpu.bitcast(a, jnp.int32)
        thr = jnp.zeros((n_exp, 1), jnp.int32)
        def count(pred):
            return jnp.sum(jnp.where(pred, 1.0, 0.0), axis=1, keepdims=True)

        for b in range(30, -1, -1):
            cand = thr | (1 << b)
            thr = jnp.where(count(bits >= cand) >= cap, cand, thr)
        gt = bits > thr
        eq = bits == thr
        need = cap - count(gt)
        pos = lax.broadcasted_iota(jnp.int32, (n_exp, n), 1)
        bound = jnp.zeros((n_exp, 1), jnp.int32)
        for b in range(int(math.log2(n)), -1, -1):
            cand = bound + (1 << b)
            bound = jnp.where(count(eq & (pos < cand)) <= need, cand, bound)
        sel = gt | (eq & (pos < bound))
        m = jnp.where(sel, 1.0, 0.0)
        m_ref[...] = m
        ri = lax.broadcasted_iota(jnp.int32, (lanes, lanes), 0)
        ci = lax.broadcasted_iota(jnp.int32, (lanes, lanes), 1)
        tri = jnp.where(ri <= ci, 1.0, 0.0).astype(BF16)
        off = jnp.zeros((n_exp, 1), F32)
        for j in range(nblk):
            cj = jnp.dot(m[:, j * lanes:(j + 1) * lanes].astype(BF16), tri,
                         preferred_element_type=F32) + off
            c_ref[:, j * lanes:(j + 1) * lanes] = cj
            off = cj[:, lanes - 1:lanes]

    pick = lax.broadcasted_iota(jnp.int32, (n_exp, n), 0) == e
    for k, src in enumerate((c_ref, m_ref, aff_ref)):
        row_ref[k:k + 1, :] = jnp.sum(jnp.where(pick, src[...], 0.0), axis=0, keepdims=True)

    rows = min(cap, 128)
    for rb in range(cap // rows):
        slot1 = (lax.broadcasted_iota(jnp.int32, (rows, lanes), 0) + (rb * rows + 1)).astype(F32)

        def body(j, carry):
            acc_i, acc_g = carry
            col = pl.multiple_of(j * lanes, lanes)
            cj = row_ref[0:1, pl.ds(col, lanes)]
            mj = row_ref[1:2, pl.ds(col, lanes)]
            aj = row_ref[2:3, pl.ds(col, lanes)]
            acc_i = acc_i + jnp.where(cj < slot1, 1.0, 0.0)
            acc_g = acc_g + jnp.where((cj == slot1) & (mj > 0.5), aj, 0.0)
            return acc_i, acc_g

        acc_i, acc_g = lax.fori_loop(
            0, nblk, body, (jnp.zeros((rows, lanes), F32), jnp.zeros((rows, lanes), F32)))
        idx_ref[rb * rows:(rb + 1) * rows, :] = jnp.sum(acc_i, axis=1, keepdims=True).astype(jnp.int32)
        gate_ref[rb * rows:(rb + 1) * rows, :] = jnp.sum(acc_g, axis=1, keepdims=True)


def _route(aff_t, col_blk0, n_set, n_sets, cap):
    n_exp = aff_t.shape[0]
    kern = functools.partial(_route_kernel, cap=cap)
    return pl.pallas_call(
        kern,
        grid=(n_sets, n_exp),
        in_specs=[pl.BlockSpec((n_exp, n_set), lambda s, e: (0, col_blk0 + s))],
        out_specs=[pl.BlockSpec((None, None, cap, 1), lambda s, e: (s, e, 0, 0)),
                   pl.BlockSpec((None, None, cap, 1), lambda s, e: (s, e, 0, 0))],
        out_shape=[jax.ShapeDtypeStruct((n_sets, n_exp, cap, 1), jnp.int32),
                   jax.ShapeDtypeStruct((n_sets, n_exp, cap, 1), F32)],
        scratch_shapes=[pltpu.VMEM((n_exp, n_set), F32), pltpu.VMEM((n_exp, n_set), F32),
                        pltpu.VMEM((8, n_set), F32)],
        compiler_params=_cparams(2),
        name="route",
    )(aff_t)


def _row_copy(src, src_row, dst, dst_row, sem):
    return pltpu.make_async_copy(src.at[pl.ds(src_row, 1), :], dst.at[pl.ds(dst_row, 1), :], sem)


def _gather_kernel(idx_ref, h_hbm, o_ref, buf, sem):
    n = buf.shape[0]
    base = pl.program_id(0) * n

    def start(r, c):
        _row_copy(h_hbm, idx_ref[base + r], buf, r, sem).start()
        return c

    def wait(r, c):
        _row_copy(h_hbm, idx_ref[base + r], buf, r, sem).wait()
        return c

    lax.fori_loop(0, n, start, 0, unroll=DMA_UNROLL)
    lax.fori_loop(0, n, wait, 0, unroll=DMA_UNROLL)
    o_ref[...] = buf[...].astype(o_ref.dtype)


def _gather_rows(idx, h, n=DMA_ROWS):
    r = idx.shape[0]
    n = min(n, r)
    d = h.shape[1]
    return pl.pallas_call(
        _gather_kernel,
        grid_spec=pltpu.PrefetchScalarGridSpec(
            num_scalar_prefetch=1,
            grid=(r // n,),
            in_specs=[pl.BlockSpec(memory_space=pl.ANY)],
            out_specs=pl.BlockSpec((n, d), lambda g, idx_ref: (g, 0)),
            scratch_shapes=[pltpu.VMEM((n, d), F32), pltpu.SemaphoreType.DMA(())]),
        out_shape=jax.ShapeDtypeStruct((r, d), BF16),
        compiler_params=_cparams(1),
        name="gather_rows",
    )(idx, h)


def _scatter_kernel(idx_ref, y_ref, gt_ref, x_in, x_hbm, buf, sem_in, sem_out):
    del x_in
    n = buf.shape[0]
    base = pl.program_id(0) * n

    def start_in(r, c):
        _row_copy(x_hbm, idx_ref[base + r], buf, r, sem_in).start()
        return c

    def wait_in(r, c):
        _row_copy(x_hbm, idx_ref[base + r], buf, r, sem_in).wait()
        return c

    def start_out(r, c):
        _row_copy(buf, r, x_hbm, idx_ref[base + r], sem_out).start()
        return c

    def wait_out(r, c):
        _row_copy(buf, r, x_hbm, idx_ref[base + r], sem_out).wait()
        return c

    lax.fori_loop(0, n, start_in, 0, unroll=DMA_UNROLL)
    lax.fori_loop(0, n, wait_in, 0, unroll=DMA_UNROLL)
    buf[...] = buf[...] + gt_ref[...] * y_ref[...]
    lax.fori_loop(0, n, start_out, 0, unroll=DMA_UNROLL)
    lax.fori_loop(0, n, wait_out, 0, unroll=DMA_UNROLL)


def _scatter_add_rows(idx, y, x, mod, k_gate, rows_per_set, n_sets, mod_row, n=DMA_ROWS):
    r, d = y.shape
    n = min(n, rows_per_set)
    steps_per_set = rows_per_set // n

    def gate_blk(g, idx_ref):
        row = (g // steps_per_set) % n_sets if mod_row is None else mod_row
        return (row * N_MOD + k_gate, 0, 0)

    return pl.pallas_call(
        _scatter_kernel,
        grid_spec=pltpu.PrefetchScalarGridSpec(
            num_scalar_prefetch=1,
            grid=(r // n,),
            in_specs=[pl.BlockSpec((n, d), lambda g, idx_ref: (g, 0)),
                      pl.BlockSpec((None, 1, d), gate_blk),
                      pl.BlockSpec(memory_space=pl.ANY)],
            out_specs=pl.BlockSpec(memory_space=pl.ANY),
            scratch_shapes=[pltpu.VMEM((n, d), F32), pltpu.SemaphoreType.DMA(()),
                            pltpu.SemaphoreType.DMA(())]),
        out_shape=jax.ShapeDtypeStruct(x.shape, F32),
        input_output_aliases={3: 0},
        compiler_params=_cparams(1),
        name="scatter_add_rows",
    )(idx, y, mod, x)


def _expert_up_kernel(x_ref, wg_ref, wu_ref, o_ref, wgs_ref, wus_ref):
    first = pl.program_id(2) == 0
    _cache_weight(wg_ref, wgs_ref, first)
    _cache_weight(wu_ref, wus_ref, first)
    x = x_ref[...]
    g = jnp.dot(x, wgs_ref[...], preferred_element_type=F32)
    u = jnp.dot(x, wus_ref[...], preferred_element_type=F32)
    o_ref[...] = (g * _sigmoid(g) * u).astype(o_ref.dtype)


def _expert_up(xin, w_g, w_u, l, tr=512, tf=256):
    n_exp, r, d = xin.shape
    ff = w_g.shape[-1]
    tr = min(tr, r)
    wspec = pl.BlockSpec((None, None, d, tf), lambda e, f, i: (l, e, 0, f))
    return pl.pallas_call(
        _expert_up_kernel,
        grid=(n_exp, ff // tf, r // tr),
        in_specs=[pl.BlockSpec((None, tr, d), lambda e, f, i: (e, i, 0)), wspec, wspec],
        out_specs=pl.BlockSpec((None, tr, tf), lambda e, f, i: (e, i, f)),
        out_shape=jax.ShapeDtypeStruct((n_exp, r, ff), BF16),
        scratch_shapes=[pltpu.VMEM((d, tf), BF16), pltpu.VMEM((d, tf), BF16)],
        compiler_params=_cparams(3),
        name="expert_up",
    )(xin, w_g, w_u)


def _expert_down_kernel(h_ref, wd_ref, gate_ref, o_ref, wds_ref):
    _cache_weight(wd_ref, wds_ref, pl.program_id(2) == 0)
    o_ref[...] = jnp.dot(h_ref[...], wds_ref[...], preferred_element_type=F32) * gate_ref[...]


def _expert_down(hid, w_d, l, gates, tr=512, tn=1024):
    n_exp, r, ff = hid.shape
    d = w_d.shape[-1]
    tr = min(tr, r)
    return pl.pallas_call(
        _expert_down_kernel,
        grid=(n_exp, d // tn, r // tr),
        in_specs=[pl.BlockSpec((None, tr, ff), lambda e, j, i: (e, i, 0)),
                  pl.BlockSpec((None, None, ff, tn), lambda e, j, i: (l, e, 0, j)),
                  pl.BlockSpec((None, tr, 1), lambda e, j, i: (e, i, 0))],
        out_specs=pl.BlockSpec((None, tr, tn), lambda e, j, i: (e, i, j)),
        out_shape=jax.ShapeDtypeStruct((n_exp, r, d), F32),
        scratch_shapes=[pltpu.VMEM((ff, tn), BF16)],
        compiler_params=_cparams(3),
        name="expert_down",
    )(hid, w_d, gates)


def _moe(x, h2, aff_t, mod, k_gate, w_g, w_u, w_d, l, row0, n_set, n_sets, mod_row):
    n_exp = aff_t.shape[0]
    d = x.shape[1]
    cap = CAPACITY_FACTOR * n_set // n_exp
    idx, gates = _route(aff_t, row0 // n_set, n_set, n_sets, cap)
    offs = row0 + n_set * jnp.arange(n_sets, dtype=jnp.int32)
    rows = (idx.reshape(n_sets, n_exp, cap) + offs[:, None, None]).transpose(1, 0, 2).reshape(-1)
    gates = gates.reshape(n_sets, n_exp, cap).transpose(1, 0, 2).reshape(n_exp, n_sets * cap, 1)
    xin = _gather_rows(rows, h2).reshape(n_exp, n_sets * cap, d)
    hid = _expert_up(xin, w_g, w_u, l)
    y = _expert_down(hid, w_d, l, gates).reshape(n_exp * n_sets * cap, d)
    return _scatter_add_rows(rows, y, x, mod, k_gate, cap, n_sets, mod_row)


def kernel(x, c, ctx, c_ctx, w_ada, b_ada, g_norm1, w_in, w_gate2, b_gate, g_head, w_branch_a,
           w_branch_b, w_out, g_norm2, w_router, w_exp_gate, w_exp_up, w_exp_down, g_final):
    bsz, n_lat, d = x.shape
    n_ctx = ctx.shape[1]
    depth = w_ada.shape[0]
    fw = w_branch_a.shape[1]
    gk = w_gate2.shape[3]
    gv = g_head.shape[1]
    rank = w_gate2.shape[2]
    dk, dv = gk // GLA_HEADS, gv // GLA_HEADS
    cgrp = fw // N_FOURIER_GROUPS
    lat_rows = bsz * n_lat
    q_col, k_col, v_col, r_col = fw, fw + gk, fw + 2 * gk, fw + 2 * gk + gv
    lr_col = r_col + gv
    gl_col = lr_col + N_DIR * rank
    lr_w = 128

    xs = jnp.concatenate([x.reshape(lat_rows, d), ctx.reshape(bsz * n_ctx, d)], axis=0)
    c_rows = jnp.concatenate([c, c_ctx[None, :], jnp.zeros((8 - bsz - 1, d), F32)], axis=0)
    mod_all = _adaln(c_rows, w_ada, b_ada)
    wc, m1, m2, mc = _dft_tables(n_lat, n_ctx, cgrp)
    zero_state = jnp.zeros((bsz, GLA_HEADS, N_DIR, dv, dk), F32)
    w_in_t = jnp.swapaxes(w_in, 1, 2)

    for l in range(depth):
        last = l == depth - 1
        mod = mod_all[l].reshape(8 * N_MOD, 1, d)
        h = _norm_mod(xs, g_norm1[l][None, :], mod, 1, 0, n_lat, bsz, BF16)
        z = _matmul_t(h, w_in_t, l, 0, lr_col,
                      tm=768 if h.shape[0] % 768 == 0 else 512)
        lr = _matmul_t(h, w_in_t, l, lr_col, lr_w)
        gl = _matmul_t(h, w_in_t, l, gl_col, 2 * d)
        p, q = _chan_dft(z, wc, fw)
        four_l = _seq_dft(p, q, m1, m2, n_lat, bsz, 1.0 / math.sqrt(n_lat * cgrp))
        four_c = _ctx_dft(p, q, mc, lat_rows // n_ctx, n_ctx, bsz, 1.0 / math.sqrt(n_ctx * cgrp))
        of_c, ob_c, s_c = _gla(z, lr, w_gate2[l], b_gate[l], zero_state, lat_rows, n_ctx, bsz,
                               q_col, k_col, v_col, dk, dv)
        of_l, ob_l, _ = _gla(z, lr, w_gate2[l], b_gate[l], s_c, 0, n_lat, bsz,
                             q_col, k_col, v_col, dk, dv)
        ogla_l = _gla_post(of_l, ob_l, z, 0, g_head[l][None, :], r_col, dv)
        ogla_c = _gla_post(of_c, ob_c, z, lat_rows, g_head[l][None, :], r_col, dv)
        m = _merge(four_l, four_c, ogla_l, ogla_c, w_branch_a, w_branch_b, l, gl)
        xs = _out_residual(m, w_out, l, xs, mod, 2, n_lat, bsz)
        wr_t = w_router[l].T.astype(BF16)
        h2, aff_t = _norm_mod_router(xs, g_norm2[l][None, :], mod, 4, 3, wr_t, n_lat, bsz)
        xs = _moe(xs, h2, aff_t, mod, 5, w_exp_gate, w_exp_up, w_exp_down, l,
                  0, n_lat, bsz, None)
        if not last:
            xs = _moe(xs, h2, aff_t, mod, 5, w_exp_gate, w_exp_up, w_exp_down, l,
                      lat_rows, n_ctx, bsz, bsz)
    return _final_norm(xs, g_final[None, :], lat_rows).reshape(bsz, n_lat, d)
```

```python
import functools
import math

import jax
import jax.numpy as jnp
from jax import lax
from jax.experimental import pallas as pl
from jax.experimental.pallas import tpu as pltpu

F32 = jnp.float32
BF16 = jnp.bfloat16

N_FOURIER_GROUPS = 4
GLA_HEADS = 4
GATE_TAU = 16.0
CHUNK = 64
N_DIR = 2
CAPACITY_FACTOR = 2
N_MOD = 6
EPS = 1e-6

LANES = 128
DFT_INNER = 128
DFT_UNROLL = 8
VMEM_LIMIT = 56 << 20
GLA_ROWS = 256
DMA_ROWS = 128
DMA_UNROLL = 8


def _cparams(n_axes):
    return pltpu.CompilerParams(dimension_semantics=("arbitrary",) * n_axes,
                                vmem_limit_bytes=VMEM_LIMIT)


def _sigmoid(v):
    return 1.0 / (1.0 + jnp.exp(-v))


def _ada_kernel(c_ref, w_ref, b_ref, o_ref):
    c = c_ref[...]
    a = (c * _sigmoid(c)).astype(BF16)
    o_ref[...] = jnp.dot(a, w_ref[...].astype(BF16), preferred_element_type=F32) + b_ref[...]


def _adaln(c_rows, w_ada, b_ada, tn=512):
    depth, d, n = w_ada.shape
    r = c_rows.shape[0]
    return pl.pallas_call(
        _ada_kernel,
        grid=(depth, n // tn),
        in_specs=[pl.BlockSpec((r, d), lambda l, j: (0, 0)),
                  pl.BlockSpec((None, d, tn), lambda l, j: (l, 0, j)),
                  pl.BlockSpec((None, 1, tn), lambda l, j: (l, 0, j))],
        out_specs=pl.BlockSpec((None, r, tn), lambda l, j: (l, 0, j)),
        out_shape=jax.ShapeDtypeStruct((depth, r, n), F32),
        compiler_params=_cparams(2),
        name="adaln",
    )(c_rows, w_ada, b_ada.reshape(depth, 1, n))


def _normed(x_ref, g_ref, sc_ref, sh_ref):
    x = x_ref[...]
    y = x * lax.rsqrt(jnp.mean(x * x, axis=-1, keepdims=True) + EPS) * g_ref[...]
    return y * (1.0 + sc_ref[...]) + sh_ref[...]


def _norm_kernel(x_ref, g_ref, sc_ref, sh_ref, o_ref):
    o_ref[...] = _normed(x_ref, g_ref, sc_ref, sh_ref).astype(o_ref.dtype)


def _norm_router_kernel(x_ref, g_ref, sc_ref, sh_ref, wr_ref, o_ref, aff_ref):
    h = _normed(x_ref, g_ref, sc_ref, sh_ref)
    tm, d = h.shape
    nch = d // LANES
    slabs = jnp.stack([h[:, c * LANES:(c + 1) * LANES] for c in range(nch)], axis=0)
    o_ref[...] = jnp.swapaxes(slabs, 0, 1).reshape(tm * nch, LANES)
    logits = lax.dot_general(wr_ref[...], h.astype(BF16), (((1,), (1,)), ((), ())),
                             preferred_element_type=F32)
    e = jnp.exp(logits - jnp.max(logits, axis=0, keepdims=True))
    aff_ref[...] = e / jnp.sum(e, axis=0, keepdims=True)


def _seg_of(i, tm, n_lat, n_seg):
    return jnp.minimum((i * tm) // n_lat, n_seg)


def _norm_mod(x, g, mod, k_scale, k_shift, n_lat, n_seg, out_dtype, tm=256):
    t, d = x.shape
    seg = functools.partial(_seg_of, tm=tm, n_lat=n_lat, n_seg=n_seg)
    return pl.pallas_call(
        _norm_kernel,
        grid=(t // tm,),
        in_specs=[pl.BlockSpec((tm, d), lambda i: (i, 0)),
                  pl.BlockSpec((1, d), lambda i: (0, 0)),
                  pl.BlockSpec((None, 1, d), lambda i: (seg(i) * N_MOD + k_scale, 0, 0)),
                  pl.BlockSpec((None, 1, d), lambda i: (seg(i) * N_MOD + k_shift, 0, 0))],
        out_specs=pl.BlockSpec((tm, d), lambda i: (i, 0)),
        out_shape=jax.ShapeDtypeStruct((t, d), out_dtype),
        compiler_params=_cparams(1),
        name="norm_mod",
    )(x, g, mod, mod)


def _norm_mod_router(x, g, mod, k_scale, k_shift, wr_t, n_lat, n_seg, tm=256):
    t, d = x.shape
    e = wr_t.shape[0]
    nch = d // LANES
    seg = functools.partial(_seg_of, tm=tm, n_lat=n_lat, n_seg=n_seg)
    return pl.pallas_call(
        _norm_router_kernel,
        grid=(t // tm,),
        in_specs=[pl.BlockSpec((tm, d), lambda i: (i, 0)),
                  pl.BlockSpec((1, d), lambda i: (0, 0)),
                  pl.BlockSpec((None, 1, d), lambda i: (seg(i) * N_MOD + k_scale, 0, 0)),
                  pl.BlockSpec((None, 1, d), lambda i: (seg(i) * N_MOD + k_shift, 0, 0)),
                  pl.BlockSpec((e, d), lambda i: (0, 0))],
        out_specs=[pl.BlockSpec((tm * nch, LANES), lambda i: (i, 0)),
                   pl.BlockSpec((e, tm), lambda i: (0, i))],
        out_shape=[jax.ShapeDtypeStruct((t * nch, LANES), F32),
                   jax.ShapeDtypeStruct((e, t), F32)],
        compiler_params=_cparams(1),
        name="norm_mod_router",
    )(x, g, mod, mod, wr_t)


def _final_norm_kernel(x_ref, g_ref, o_ref):
    x = x_ref[...]
    o_ref[...] = x * lax.rsqrt(jnp.mean(x * x, axis=-1, keepdims=True) + EPS) * g_ref[...]


def _final_norm(x, g, rows, tm=256):
    d = x.shape[1]
    return pl.pallas_call(
        _final_norm_kernel,
        grid=(rows // tm,),
        in_specs=[pl.BlockSpec((tm, d), lambda i: (i, 0)),
                  pl.BlockSpec((1, d), lambda i: (0, 0))],
        out_specs=pl.BlockSpec((tm, d), lambda i: (i, 0)),
        out_shape=jax.ShapeDtypeStruct((rows, d), F32),
        compiler_params=_cparams(1),
        name="final_norm",
    )(x, g)


def _cache_weight(w_ref, ws_ref, first):
    @pl.when(first)
    def _():
        ws_ref[...] = w_ref[...].astype(BF16)


def _mm_t_kernel(a_ref, bt_ref, o_ref, bs_ref):
    @pl.when(pl.program_id(1) == 0)
    def _():
        bs_ref[...] = bt_ref[0].T.astype(BF16)

    o_ref[...] = jnp.dot(a_ref[...], bs_ref[...], preferred_element_type=F32).astype(o_ref.dtype)


def _layer_spec(block, index_map, w, l):
    if w.ndim == len(block):
        return pl.BlockSpec(block, index_map)
    return pl.BlockSpec((None,) + block, lambda *g: (l,) + index_map(*g))


def _matmul_t(a, bt, l, col0, ncols, out_dtype=F32, tm=512, tn=512):
    t, k = a.shape
    tn = min(tn, ncols)
    assert col0 % 8 == 0 and t % tm == 0 and ncols % tn == 0
    return pl.pallas_call(
        _mm_t_kernel,
        grid=(ncols // tn, t // tm),
        in_specs=[pl.BlockSpec((tm, k), lambda j, i: (i, 0)),
                  pl.BlockSpec((pl.Element(1), pl.Element(tn), pl.Element(k)),
                               lambda j, i: (l, pl.multiple_of(col0 + j * tn, 8), 0))],
        out_specs=pl.BlockSpec((tm, tn), lambda j, i: (i, j)),
        out_shape=jax.ShapeDtypeStruct((t, ncols), out_dtype),
        scratch_shapes=[pltpu.VMEM((k, tn), BF16)],
        compiler_params=_cparams(2),
        name="matmul_t",
    )(a, bt)


def _merge_kernel(fl_ref, fc_ref, ol_ref, oc_ref, wa_ref, wb_ref, ga_ref, gb_ref, o_ref,
                  was_ref, wbs_ref, *, lat_tiles):
    i = pl.program_id(1)
    _cache_weight(wa_ref, was_ref, i == 0)
    _cache_weight(wb_ref, wbs_ref, i == 0)

    def compute(fa_ref, oa_ref):
        ya = jnp.dot(fa_ref[...], was_ref[...], preferred_element_type=F32)
        yb = jnp.dot(oa_ref[...], wbs_ref[...], preferred_element_type=F32)
        o_ref[...] = (_sigmoid(ga_ref[...]) * ya + _sigmoid(gb_ref[...]) * yb).astype(o_ref.dtype)

    @pl.when(i < lat_tiles)
    def _():
        compute(fl_ref, ol_ref)

    @pl.when(i >= lat_tiles)
    def _():
        compute(fc_ref, oc_ref)


def _merge(four_lat, four_ctx, ogla_lat, ogla_ctx, w_a, w_b, l, gates, tm=512, tn=512):
    ka, kb = four_lat.shape[1], ogla_lat.shape[1]
    lat_tiles = four_lat.shape[0] // tm
    t = gates.shape[0]
    d = w_a.shape[-1]
    nj = d // tn

    def lat(j, i):
        return (jnp.minimum(i, lat_tiles - 1), 0)

    def ctx(j, i):
        return (jnp.maximum(i - lat_tiles, 0), 0)

    return pl.pallas_call(
        functools.partial(_merge_kernel, lat_tiles=lat_tiles),
        grid=(nj, t // tm),
        in_specs=[pl.BlockSpec((tm, ka), lat),
                  pl.BlockSpec((tm, ka), ctx),
                  pl.BlockSpec((tm, kb), lat),
                  pl.BlockSpec((tm, kb), ctx),
                  _layer_spec((ka, tn), lambda j, i: (0, j), w_a, l),
                  _layer_spec((kb, tn), lambda j, i: (0, j), w_b, l),
                  pl.BlockSpec((tm, tn), lambda j, i: (i, j)),
                  pl.BlockSpec((tm, tn), lambda j, i: (i, nj + j))],
        out_specs=pl.BlockSpec((tm, tn), lambda j, i: (i, j)),
        out_shape=jax.ShapeDtypeStruct((t, d), BF16),
        scratch_shapes=[pltpu.VMEM((ka, tn), BF16), pltpu.VMEM((kb, tn), BF16)],
        compiler_params=_cparams(2),
        name="merge",
    )(four_lat, four_ctx, ogla_lat, ogla_ctx, w_a, w_b, gates, gates)


def _out_res_kernel(a_ref, b_ref, x_ref, gt_ref, o_ref, bs_ref):
    _cache_weight(b_ref, bs_ref, pl.program_id(1) == 0)
    y = jnp.dot(a_ref[...], bs_ref[...], preferred_element_type=F32)
    o_ref[...] = x_ref[...] + gt_ref[...] * y


def _out_residual(m, w_out, l, x, mod, k_gate, n_lat, n_seg, tm=512, tn=512):
    t, k = m.shape
    d = w_out.shape[-1]
    seg = functools.partial(_seg_of, tm=tm, n_lat=n_lat, n_seg=n_seg)
    return pl.pallas_call(
        _out_res_kernel,
        grid=(d // tn, t // tm),
        in_specs=[pl.BlockSpec((tm, k), lambda j, i: (i, 0)),
                  _layer_spec((k, tn), lambda j, i: (0, j), w_out, l),
                  pl.BlockSpec((tm, tn), lambda j, i: (i, j)),
                  pl.BlockSpec((None, 1, tn), lambda j, i: (seg(i) * N_MOD + k_gate, 0, j))],
        out_specs=pl.BlockSpec((tm, tn), lambda j, i: (i, j)),
        out_shape=jax.ShapeDtypeStruct((t, d), F32),
        scratch_shapes=[pltpu.VMEM((k, tn), BF16)],
        compiler_params=_cparams(2),
        name="out_residual",
    )(m, w_out, x, mod)


def _chan_dft_kernel(u_ref, w_ref, p_ref, q_ref):
    c = p_ref.shape[1]
    y = jnp.dot(u_ref[...].astype(BF16), w_ref[...], preferred_element_type=F32)
    p_ref[...] = y[:, :c]
    q_ref[...] = y[:, c:]


def _chan_dft(z, wc, width, tm=512):
    t = z.shape[0]
    c = wc.shape[0]
    return pl.pallas_call(
        _chan_dft_kernel,
        grid=(t // tm, width // c),
        in_specs=[pl.BlockSpec((tm, c), lambda i, g: (i, g)),
                  pl.BlockSpec((c, 2 * c), lambda i, g: (0, 0))],
        out_specs=[pl.BlockSpec((tm, c), lambda i, g: (i, g)),
                   pl.BlockSpec((tm, c), lambda i, g: (i, g))],
        out_shape=[jax.ShapeDtypeStruct((t, width), F32)] * 2,
        compiler_params=_cparams(2),
        name="chan_dft",
    )(z, wc)


def _seq_dft_kernel(p_ref, q_ref, m1_ref, m2_ref, o_ref, bs_ref, os_ref, *, n1, scale):
    n2 = DFT_INNER

    def stage1(i, carry):
        xp = p_ref[pl.ds(i, n2, stride=n1), :]
        xq = q_ref[pl.ds(i, n2, stride=n1), :]
        xs = jnp.concatenate([xp, xq], axis=0).astype(BF16)
        row0 = pl.multiple_of(i * (2 * n2), 2 * n2)
        bs_ref[pl.ds(row0, 2 * n2), :] = jnp.dot(m1_ref[i], xs, preferred_element_type=F32)
        return carry

    lax.fori_loop(0, n1, stage1, 0, unroll=DFT_UNROLL)

    def stage2(k2, carry):
        are = bs_ref[pl.ds(k2, n1, stride=2 * n2), :]
        aim = bs_ref[pl.ds(n2 + k2, n1, stride=2 * n2), :]
        a = jnp.concatenate([are, aim], axis=0).astype(BF16)
        y = jnp.dot(m2_ref[...], a, preferred_element_type=F32) * scale
        os_ref[pl.ds(k2, n1, stride=n2), :] = y
        return carry

    lax.fori_loop(0, n2, stage2, 0, unroll=DFT_UNROLL)
    o_ref[...] = os_ref[...].astype(o_ref.dtype)


def _seq_dft(p, q, m1, m2, n_seq, n_seg, scale, ft=128):
    t, w = p.shape
    n1 = n_seq // DFT_INNER
    kern = functools.partial(_seq_dft_kernel, n1=n1, scale=scale)
    return pl.pallas_call(
        kern,
        grid=(n_seg, w // ft),
        in_specs=[pl.BlockSpec((n_seq, ft), lambda b, j: (b, j)),
                  pl.BlockSpec((n_seq, ft), lambda b, j: (b, j)),
                  pl.BlockSpec(memory_space=pltpu.VMEM),
                  pl.BlockSpec(memory_space=pltpu.VMEM)],
        out_specs=pl.BlockSpec((n_seq, ft), lambda b, j: (b, j)),
        out_shape=jax.ShapeDtypeStruct((n_seg * n_seq, w), BF16),
        scratch_shapes=[pltpu.VMEM((n1 * 2 * DFT_INNER, ft), F32), pltpu.VMEM((n_seq, ft), F32)],
        compiler_params=_cparams(2),
        name="seq_dft",
    )(p, q, m1, m2)


def _ctx_dft_kernel(p_ref, q_ref, m_ref, o_ref, *, scale):
    a = jnp.concatenate([p_ref[...], q_ref[...]], axis=0).astype(BF16)
    o_ref[...] = (jnp.dot(m_ref[...], a, preferred_element_type=F32) * scale).astype(o_ref.dtype)


def _ctx_dft(p, q, mc, row_blk0, n_ctx, n_seg, scale, ft=512):
    w = p.shape[1]
    kern = functools.partial(_ctx_dft_kernel, scale=scale)
    spec = pl.BlockSpec((n_ctx, ft), lambda b, j: (row_blk0 + b, j))
    return pl.pallas_call(
        kern,
        grid=(n_seg, w // ft),
        in_specs=[spec, spec, pl.BlockSpec((n_ctx, 2 * n_ctx), lambda b, j: (0, 0))],
        out_specs=pl.BlockSpec((n_ctx, ft), lambda b, j: (b, j)),
        out_shape=jax.ShapeDtypeStruct((n_seg * n_ctx, w), BF16),
        compiler_params=_cparams(2),
        name="ctx_dft",
    )(p, q, mc)


def _dft_tables(n_seq, n_ctx, c):
    def cs(num, den):
        ang = (2.0 * math.pi / den) * (num % den).astype(F32)
        return jnp.cos(ang), jnp.sin(ang)

    ic = jnp.arange(c, dtype=jnp.int32)
    cc, sc = cs(ic[:, None] * ic[None, :], c)
    wc = jnp.concatenate([cc, -sc], axis=1).astype(BF16)

    n2 = DFT_INNER
    n1 = n_seq // n2
    i1 = jnp.arange(n1, dtype=jnp.int32)[:, None, None]
    k2 = jnp.arange(n2, dtype=jnp.int32)[None, :, None]
    j2 = jnp.arange(n2, dtype=jnp.int32)[None, None, :]
    c1, s1 = cs((i1 + n1 * j2) * k2, n_seq)
    m1 = jnp.concatenate([jnp.concatenate([c1, s1], axis=2),
                          jnp.concatenate([-s1, c1], axis=2)], axis=1).astype(BF16)
    k1 = jnp.arange(n1, dtype=jnp.int32)
    c2, s2 = cs(k1[:, None] * k1[None, :], n1)
    m2 = jnp.concatenate([c2, s2], axis=1).astype(BF16)

    il = jnp.arange(n_ctx, dtype=jnp.int32)
    cl, sl = cs(il[:, None] * il[None, :], n_ctx)
    mc = jnp.concatenate([cl, sl], axis=1).astype(BF16)
    return wc, m1, m2, mc


def _exact_tri_matmul(tri, v):
    h1 = v.astype(BF16)
    r1 = v - h1.astype(F32)
    h2 = r1.astype(BF16)
    h3 = (r1 - h2.astype(F32)).astype(BF16)
    return (jnp.dot(tri, h1, preferred_element_type=F32)
            + jnp.dot(tri, h2, preferred_element_type=F32)
            + jnp.dot(tri, h3, preferred_element_type=F32))


def _gla_kernel(qf_ref, kf_ref, vf_ref, lrf_ref, qb_ref, kb_ref, vb_ref, lrb_ref, wg_ref, bg_ref,
                s0_ref, of_ref, ob_ref, sout_ref, st_ref, *, rank, q_scale):
    t = pl.program_id(2)
    rows = qf_ref.shape[0]
    n_chunks = rows // CHUNK

    @pl.when(t == 0)
    def _():
        st_ref[...] = s0_ref[...]

    ii = lax.broadcasted_iota(jnp.int32, (rows, rows), 0)
    jj = lax.broadcasted_iota(jnp.int32, (rows, rows), 1)
    same_chunk = (ii // CHUNK) == (jj // CHUNK)
    dirs = ((qf_ref, kf_ref, vf_ref, lrf_ref, of_ref, same_chunk & (jj <= ii)),
            (qb_ref, kb_ref, vb_ref, lrb_ref, ob_ref, same_chunk & (jj >= ii)))

    pre = []
    for d, (q_ref, k_ref, v_ref, lr_ref, _, keep) in enumerate(dirs):
        lr = lr_ref[:, d * rank:(d + 1) * rank].astype(BF16)
        gl = jnp.dot(lr, wg_ref[d].astype(BF16), preferred_element_type=F32) + bg_ref[d]
        logg = (jnp.minimum(gl, 0.0) - jnp.log1p(jnp.exp(-jnp.abs(gl)))) * (1.0 / GATE_TAU)
        tri = jnp.where(keep, 1.0, 0.0).astype(BF16)
        cum = _exact_tri_matmul(tri, logg)
        tots = [cum[j * CHUNK:j * CHUNK + 1, :] if d else cum[(j + 1) * CHUNK - 1:(j + 1) * CHUNK, :]
                for j in range(n_chunks)]
        tot_rows = jnp.concatenate([jnp.broadcast_to(t_, (CHUNK, t_.shape[1])) for t_ in tots], axis=0)
        q = q_ref[...] * q_scale
        k = k_ref[...]
        vb = v_ref[...].astype(BF16)
        qd = (q * jnp.exp(cum)).astype(BF16)
        ki = (k * jnp.exp(-cum)).astype(BF16)
        kt = (k * jnp.exp(tot_rows - cum)).astype(BF16)
        att = lax.dot_general(qd, ki, (((1,), (1,)), ((), ())), preferred_element_type=F32)
        att = jnp.where(keep, att, 0.0).astype(BF16)
        o_intra = jnp.dot(att, vb, preferred_element_type=F32)
        pre.append((qd, kt, tots, o_intra))

    def chunk(d, j):
        v_ref, o_ref = dirs[d][2], dirs[d][4]
        qd, kt, tots, o_intra = pre[d]
        rs = slice(j * CHUNK, (j + 1) * CHUNK)
        st = st_ref[d]
        o_inter = lax.dot_general(qd[rs, :], st.astype(BF16), (((1,), (1,)), ((), ())),
                                  preferred_element_type=F32)
        o_ref[rs, :] = o_intra[rs, :] + o_inter
        upd = jnp.dot(v_ref[rs, :].T.astype(BF16), kt[rs, :], preferred_element_type=F32)
        st_ref[d] = st * jnp.exp(tots[j]) + upd

    for j in range(n_chunks):
        chunk(0, j)
        chunk(1, n_chunks - 1 - j)

    @pl.when(t == pl.num_programs(2) - 1)
    def _():
        sout_ref[...] = st_ref[...]


def _gla(z, lr, w_gate2, b_gate, s0, row0, seg_len, n_seg, q_col, k_col, v_col, dk, dv):
    heads = GLA_HEADS
    rank = w_gate2.shape[1]
    rows = GLA_ROWS
    nt = seg_len // rows
    blk0 = row0 // rows

    def fwd(b, s):
        return b * nt + s

    def bwd(b, s):
        return b * nt + nt - 1 - s

    def zspec(width, col, blk):
        return pl.BlockSpec((rows, width), lambda b, h, s: (blk0 + blk(b, s), col // width + h))

    def lrspec(blk):
        return pl.BlockSpec((rows, lr.shape[1]), lambda b, h, s: (blk0 + blk(b, s), 0))

    state = pl.BlockSpec((None, None, N_DIR, dv, dk), lambda b, h, s: (b, h, 0, 0, 0))
    kern = functools.partial(_gla_kernel, rank=rank, q_scale=dk ** -0.5)
    return pl.pallas_call(
        kern,
        grid=(n_seg, heads, nt),
        in_specs=[zspec(dk, q_col, fwd), zspec(dk, k_col, fwd), zspec(dv, v_col, fwd), lrspec(fwd),
                  zspec(dk, q_col, bwd), zspec(dk, k_col, bwd), zspec(dv, v_col, bwd), lrspec(bwd),
                  pl.BlockSpec((N_DIR, rank, dk), lambda b, h, s: (0, 0, h)),
                  pl.BlockSpec((N_DIR, 1, dk), lambda b, h, s: (0, 0, h)),
                  state],
        out_specs=[pl.BlockSpec((rows, dv), lambda b, h, s: (fwd(b, s), h)),
                   pl.BlockSpec((rows, dv), lambda b, h, s: (bwd(b, s), h)),
                   state],
        out_shape=[jax.ShapeDtypeStruct((n_seg * seg_len, heads * dv), F32),
                   jax.ShapeDtypeStruct((n_seg * seg_len, heads * dv), F32),
                   jax.ShapeDtypeStruct((n_seg, heads, N_DIR, dv, dk), F32)],
        scratch_shapes=[pltpu.VMEM((N_DIR, dv, dk), F32)],
        compiler_params=_cparams(3),
        name="gla_scan",
    )(z, z, z, lr, z, z, z, lr, w_gate2, b_gate.reshape(N_DIR, 1, -1), s0)


def _gla_post_kernel(of_ref, ob_ref, r_ref, g_ref, o_ref, *, dv):
    o = of_ref[...] + ob_ref[...]
    r = r_ref[...]
    gate = g_ref[...] * (r * _sigmoid(r))
    for h in range(o.shape[1] // dv):
        cs = slice(h * dv, (h + 1) * dv)
        oh = o[:, cs]
        oh = oh * lax.rsqrt(jnp.mean(oh * oh, axis=-1, keepdims=True) + EPS)
        o_ref[:, cs] = (oh * gate[:, cs]).astype(o_ref.dtype)


def _gla_post(o_f, o_b, z, row0, g_head, r_col, dv, tm=256):
    t, gv = o_f.shape
    blk0 = row0 // tm
    kern = functools.partial(_gla_post_kernel, dv=dv)
    return pl.pallas_call(
        kern,
        grid=(t // tm,),
        in_specs=[pl.BlockSpec((tm, gv), lambda i: (i, 0)),
                  pl.BlockSpec((tm, gv), lambda i: (i, 0)),
                  pl.BlockSpec((tm, gv), lambda i: (blk0 + i, r_col // gv)),
                  pl.BlockSpec((1, gv), lambda i: (0, 0))],
        out_specs=pl.BlockSpec((tm, gv), lambda i: (i, 0)),
        out_shape=jax.ShapeDtypeStruct((t, gv), BF16),
        compiler_params=_cparams(1),
        name="gla_post",
    )(o_f, o_b, z, g_head)


_ROUTE_COLS = 16


def _route_kernel(aff_ref, idx_ref, gate_ref, c_ref, m_ref, row_ref, res_ref, win_ref, w0_ref,
                  *, cap):
    e = pl.program_id(1)
    n_exp, n = aff_ref.shape
    lanes = 128
    nblk = n // lanes

    @pl.when(e == 0)
    def _():
        a = aff_ref[...]
        bits = pltpu.bitcast(a, jnp.int32)
        thr = jnp.zeros((n_exp, 1), jnp.int32)
        def count(pred):
            return jnp.sum(jnp.where(pred, 1.0, 0.0), axis=1, keepdims=True)

        for b in range(30, -1, -1):
            cand = thr | (1 << b)
            thr = jnp.where(count(bits >= cand) >= cap, cand, thr)
        gt = bits > thr
        eq = bits == thr
        need = cap - count(gt)
        pos = lax.broadcasted_iota(jnp.int32, (n_exp, n), 1)
        bound = jnp.zeros((n_exp, 1), jnp.int32)
        for b in range(int(math.log2(n)), -1, -1):
            cand = bound + (1 << b)
            bound = jnp.where(count(eq & (pos < cand)) <= need, cand, bound)
        sel = gt | (eq & (pos < bound))
        m = jnp.where(sel, 1.0, 0.0)
        m_ref[...] = m
        ri = lax.broadcasted_iota(jnp.int32, (lanes, lanes), 0)
        ci = lax.broadcasted_iota(jnp.int32, (lanes, lanes), 1)
        tri = jnp.where(ri <= ci, 1.0, 0.0).astype(BF16)
        off = jnp.zeros((n_exp, 1), F32)
        for j in range(nblk):
            cj = jnp.dot(m[:, j * lanes:(j + 1) * lanes].astype(BF16), tri,
                         preferred_element_type=F32) + off
            c_ref[:, j * lanes:(j + 1) * lanes] = cj
            off = cj[:, lanes - 1:lanes]

    pick = lax.broadcasted_iota(jnp.int32, (n_exp, n), 0) == e
    for k, src in enumerate((c_ref, m_ref, aff_ref)):
        row_ref[k:k + 1, :] = jnp.sum(jnp.where(pick, src[...], 0.0), axis=0, keepdims=True)

    win = lanes + 8
    res_ref[...] = jnp.zeros_like(res_ref)
    lane_row = lax.broadcasted_iota(jnp.int32, (1, lanes), 1).astype(F32).astype(BF16)
    pad_rows = jnp.zeros((_ROUTE_COLS - 5, lanes), BF16)
    row_in_win = lax.broadcasted_iota(jnp.int32, (win, lanes), 0)

    def match(j, carry):
        col = pl.multiple_of(j * lanes, lanes)
        cj = row_ref[0:1, pl.ds(col, lanes)]
        mj = row_ref[1:2, pl.ds(col, lanes)]
        aj = row_ref[2:3, pl.ds(col, lanes)]
        before = (cj[0, 0] - mj[0, 0]).astype(jnp.int32)
        w0 = (before // 8) * 8
        slot1 = (row_in_win + (w0 + 1)).astype(F32)
        onehot = jnp.where((cj == slot1) & (mj > 0.5), 1.0, 0.0).astype(BF16)
        a1 = aj.astype(BF16)
        r1 = aj - a1.astype(F32)
        a2 = r1.astype(BF16)
        a3 = (r1 - a2.astype(F32)).astype(BF16)
        blk_row = jnp.full((1, lanes), j, F32).astype(BF16)
        vals = jnp.concatenate([lane_row, blk_row, a1, a2, a3, pad_rows], axis=0)
        y = lax.dot_general(onehot, vals, (((1,), (1,)), ((), ())), preferred_element_type=F32)
        win_ref[pl.ds(pl.multiple_of(j * win, 8), win), 0:_ROUTE_COLS] = y
        w0_ref[j] = w0
        return carry

    def place(j, carry):
        w0 = pl.multiple_of(w0_ref[j], 8)
        y = win_ref[pl.ds(pl.multiple_of(j * win, 8), win), 0:_ROUTE_COLS]
        res_ref[pl.ds(w0, win), 0:_ROUTE_COLS] = res_ref[pl.ds(w0, win), 0:_ROUTE_COLS] + y
        return carry

    lax.fori_loop(0, nblk, match, 0, unroll=min(4, nblk))
    lax.fori_loop(0, nblk, place, 0, unroll=2)
    res = res_ref[0:cap, :]
    lane = lax.broadcasted_iota(jnp.int32, res.shape, 1)
    tok = jnp.where(lane == 0, res, 0.0) + jnp.where(lane == 1, res * lanes, 0.0)
    idx_ref[...] = jnp.sum(tok, axis=1, keepdims=True).astype(jnp.int32)
    gate_ref[...] = jnp.sum(jnp.where((lane >= 2) & (lane <= 4), res, 0.0), axis=1, keepdims=True)


def _route(aff_t, col_blk0, n_set, n_sets, cap):
    n_exp = aff_t.shape[0]
    kern = functools.partial(_route_kernel, cap=cap)
    return pl.pallas_call(
        kern,
        grid=(n_sets, n_exp),
        in_specs=[pl.BlockSpec((n_exp, n_set), lambda s, e: (0, col_blk0 + s))],
        out_specs=[pl.BlockSpec((None, None, cap, 1), lambda s, e: (s, e, 0, 0)),
                   pl.BlockSpec((None, None, cap, 1), lambda s, e: (s, e, 0, 0))],
        out_shape=[jax.ShapeDtypeStruct((n_sets, n_exp, cap, 1), jnp.int32),
                   jax.ShapeDtypeStruct((n_sets, n_exp, cap, 1), F32)],
        scratch_shapes=[pltpu.VMEM((n_exp, n_set), F32), pltpu.VMEM((n_exp, n_set), F32),
                        pltpu.VMEM((8, n_set), F32), pltpu.VMEM((cap + 128 + 8, 128), F32),
                        pltpu.VMEM((n_set // 128 * (128 + 8), 128), F32),
                        pltpu.SMEM((n_set // 128,), jnp.int32)],
        compiler_params=_cparams(2),
        name="route",
    )(aff_t)


def _row_copy(src, src_row, dst, dst_row, sem):
    return pltpu.make_async_copy(src.at[pl.ds(src_row, 1), :], dst.at[pl.ds(dst_row, 1), :], sem)


def _gather_kernel(idx_ref, h_hbm, o_ref, buf, sem):
    n, d = o_ref.shape
    nch = d // LANES
    base = pl.program_id(0) * n

    def token_copy(r):
        src = pl.multiple_of(idx_ref[base + r] * nch, nch)
        dst = pl.multiple_of(r * nch, nch)
        return pltpu.make_async_copy(h_hbm.at[pl.ds(src, nch), :], buf.at[pl.ds(dst, nch), :], sem)

    def start(r, c):
        token_copy(r).start()
        return c

    def wait(r, c):
        token_copy(r).wait()
        return c

    lax.fori_loop(0, n, start, 0, unroll=DMA_UNROLL)
    lax.fori_loop(0, n, wait, 0, unroll=DMA_UNROLL)
    slabs = jnp.swapaxes(buf[...].reshape(n, nch, LANES), 0, 1)
    for c in range(nch):
        o_ref[:, c * LANES:(c + 1) * LANES] = slabs[c].astype(o_ref.dtype)


def _gather_rows(idx, h, d, n=DMA_ROWS):
    r = idx.shape[0]
    n = min(n, r)
    return pl.pallas_call(
        _gather_kernel,
        grid_spec=pltpu.PrefetchScalarGridSpec(
            num_scalar_prefetch=1,
            grid=(r // n,),
            in_specs=[pl.BlockSpec(memory_space=pl.ANY)],
            out_specs=pl.BlockSpec((n, d), lambda g, idx_ref: (g, 0)),
            scratch_shapes=[pltpu.VMEM((n * d // LANES, LANES), F32), pltpu.SemaphoreType.DMA(())]),
        out_shape=jax.ShapeDtypeStruct((r, d), BF16),
        compiler_params=_cparams(1),
        name="gather_rows",
    )(idx, h)


def _scatter_kernel(idx_ref, y_ref, gt_ref, x_in, x_hbm, buf, sem_in, sem_out):
    del x_in
    n = buf.shape[0]
    base = pl.program_id(0) * n

    def start_in(r, c):
        _row_copy(x_hbm, idx_ref[base + r], buf, r, sem_in).start()
        return c

    def wait_in(r, c):
        _row_copy(x_hbm, idx_ref[base + r], buf, r, sem_in).wait()
        return c

    def start_out(r, c):
        _row_copy(buf, r, x_hbm, idx_ref[base + r], sem_out).start()
        return c

    def wait_out(r, c):
        _row_copy(buf, r, x_hbm, idx_ref[base + r], sem_out).wait()
        return c

    lax.fori_loop(0, n, start_in, 0, unroll=DMA_UNROLL)
    lax.fori_loop(0, n, wait_in, 0, unroll=DMA_UNROLL)
    buf[...] = buf[...] + gt_ref[...] * y_ref[...]
    lax.fori_loop(0, n, start_out, 0, unroll=DMA_UNROLL)
    lax.fori_loop(0, n, wait_out, 0, unroll=DMA_UNROLL)


def _scatter_add_rows(idx, y, x, mod, k_gate, rows_per_set, n_sets, mod_row, n=DMA_ROWS):
    r, d = y.shape
    n = min(n, rows_per_set)
    steps_per_set = rows_per_set // n

    def gate_blk(g, idx_ref):
        row = (g // steps_per_set) % n_sets if mod_row is None else mod_row
        return (row * N_MOD + k_gate, 0, 0)

    return pl.pallas_call(
        _scatter_kernel,
        grid_spec=pltpu.PrefetchScalarGridSpec(
            num_scalar_prefetch=1,
            grid=(r // n,),
            in_specs=[pl.BlockSpec((n, d), lambda g, idx_ref: (g, 0)),
                      pl.BlockSpec((None, 1, d), gate_blk),
                      pl.BlockSpec(memory_space=pl.ANY)],
            out_specs=pl.BlockSpec(memory_space=pl.ANY),
            scratch_shapes=[pltpu.VMEM((n, d), F32), pltpu.SemaphoreType.DMA(()),
                            pltpu.SemaphoreType.DMA(())]),
        out_shape=jax.ShapeDtypeStruct(x.shape, F32),
        input_output_aliases={3: 0},
        compiler_params=_cparams(1),
        name="scatter_add_rows",
    )(idx, y, mod, x)


def _expert_up_kernel(x_ref, wg_ref, wu_ref, o_ref, wgs_ref, wus_ref):
    first = pl.program_id(2) == 0
    _cache_weight(wg_ref, wgs_ref, first)
    _cache_weight(wu_ref, wus_ref, first)
    x = x_ref[...]
    g = jnp.dot(x, wgs_ref[...], preferred_element_type=F32)
    u = jnp.dot(x, wus_ref[...], preferred_element_type=F32)
    o_ref[...] = (g * _sigmoid(g) * u).astype(o_ref.dtype)


def _expert_up(xin, w_g, w_u, l, tr=1024, tf=256):
    n_exp, r, d = xin.shape
    ff = w_g.shape[-1]
    tr = min(tr, r)
    wspec = pl.BlockSpec((None, None, d, tf), lambda e, f, i: (l, e, 0, f))
    return pl.pallas_call(
        _expert_up_kernel,
        grid=(n_exp, ff // tf, r // tr),
        in_specs=[pl.BlockSpec((None, tr, d), lambda e, f, i: (e, i, 0)), wspec, wspec],
        out_specs=pl.BlockSpec((None, tr, tf), lambda e, f, i: (e, i, f)),
        out_shape=jax.ShapeDtypeStruct((n_exp, r, ff), BF16),
        scratch_shapes=[pltpu.VMEM((d, tf), BF16), pltpu.VMEM((d, tf), BF16)],
        compiler_params=_cparams(3),
        name="expert_up",
    )(xin, w_g, w_u)


def _expert_down_kernel(h_ref, wd_ref, gate_ref, o_ref, wds_ref):
    _cache_weight(wd_ref, wds_ref, pl.program_id(2) == 0)
    o_ref[...] = jnp.dot(h_ref[...], wds_ref[...], preferred_element_type=F32) * gate_ref[...]


def _expert_down(hid, w_d, l, gates, tr=1024, tn=1024):
    n_exp, r, ff = hid.shape
    d = w_d.shape[-1]
    tr = min(tr, r)
    return pl.pallas_call(
        _expert_down_kernel,
        grid=(n_exp, d // tn, r // tr),
        in_specs=[pl.BlockSpec((None, tr, ff), lambda e, j, i: (e, i, 0)),
                  pl.BlockSpec((None, None, ff, tn), lambda e, j, i: (l, e, 0, j)),
                  pl.BlockSpec((None, tr, 1), lambda e, j, i: (e, i, 0))],
        out_specs=pl.BlockSpec((None, tr, tn), lambda e, j, i: (e, i, j)),
        out_shape=jax.ShapeDtypeStruct((n_exp, r, d), F32),
        scratch_shapes=[pltpu.VMEM((ff, tn), BF16)],
        compiler_params=_cparams(3),
        name="expert_down",
    )(hid, w_d, gates)


def _moe(x, h2, aff_t, mod, k_gate, w_g, w_u, w_d, l, row0, n_set, n_sets, mod_row):
    n_exp = aff_t.shape[0]
    d = x.shape[1]
    cap = CAPACITY_FACTOR * n_set // n_exp
    idx, gates = _route(aff_t, row0 // n_set, n_set, n_sets, cap)
    offs = row0 + n_set * jnp.arange(n_sets, dtype=jnp.int32)
    rows = (idx.reshape(n_sets, n_exp, cap) + offs[:, None, None]).transpose(1, 0, 2).reshape(-1)
    gates = gates.reshape(n_sets, n_exp, cap).transpose(1, 0, 2).reshape(n_exp, n_sets * cap, 1)
    xin = _gather_rows(rows, h2, d).reshape(n_exp, n_sets * cap, d)
    hid = _expert_up(xin, w_g, w_u, l)
    y = _expert_down(hid, w_d, l, gates).reshape(n_exp * n_sets * cap, d)
    return _scatter_add_rows(rows, y, x, mod, k_gate, cap, n_sets, mod_row)


def kernel(x, c, ctx, c_ctx, w_ada, b_ada, g_norm1, w_in, w_gate2, b_gate, g_head, w_branch_a,
           w_branch_b, w_out, g_norm2, w_router, w_exp_gate, w_exp_up, w_exp_down, g_final):
    bsz, n_lat, d = x.shape
    n_ctx = ctx.shape[1]
    depth = w_ada.shape[0]
    fw = w_branch_a.shape[1]
    gk = w_gate2.shape[3]
    gv = g_head.shape[1]
    rank = w_gate2.shape[2]
    dk, dv = gk // GLA_HEADS, gv // GLA_HEADS
    cgrp = fw // N_FOURIER_GROUPS
    lat_rows = bsz * n_lat
    q_col, k_col, v_col, r_col = fw, fw + gk, fw + 2 * gk, fw + 2 * gk + gv
    lr_col = r_col + gv
    gl_col = lr_col + N_DIR * rank
    lr_w = 128

    xs = jnp.concatenate([x.reshape(lat_rows, d), ctx.reshape(bsz * n_ctx, d)], axis=0)
    c_rows = jnp.concatenate([c, c_ctx[None, :], jnp.zeros((8 - bsz - 1, d), F32)], axis=0)
    mod_all = _adaln(c_rows, w_ada, b_ada)
    wc, m1, m2, mc = _dft_tables(n_lat, n_ctx, cgrp)
    zero_state = jnp.zeros((bsz, GLA_HEADS, N_DIR, dv, dk), F32)
    w_in_t = jnp.swapaxes(w_in, 1, 2)

    for l in range(depth):
        last = l == depth - 1
        mod = mod_all[l].reshape(8 * N_MOD, 1, d)
        h = _norm_mod(xs, g_norm1[l][None, :], mod, 1, 0, n_lat, bsz, BF16)
        tm_in = 768 if h.shape[0] % 768 == 0 else 512
        z = _matmul_t(h, w_in_t, l, 0, lr_col, tm=tm_in)
        lr = _matmul_t(h, w_in_t, l, lr_col, lr_w)
        gl = _matmul_t(h, w_in_t, l, gl_col, 2 * d, tm=tm_in)
        p, q = _chan_dft(z, wc, fw)
        four_l = _seq_dft(p, q, m1, m2, n_lat, bsz, 1.0 / math.sqrt(n_lat * cgrp))
        four_c = _ctx_dft(p, q, mc, lat_rows // n_ctx, n_ctx, bsz, 1.0 / math.sqrt(n_ctx * cgrp))
        of_c, ob_c, s_c = _gla(z, lr, w_gate2[l], b_gate[l], zero_state, lat_rows, n_ctx, bsz,
                               q_col, k_col, v_col, dk, dv)
        of_l, ob_l, _ = _gla(z, lr, w_gate2[l], b_gate[l], s_c, 0, n_lat, bsz,
                             q_col, k_col, v_col, dk, dv)
        ogla_l = _gla_post(of_l, ob_l, z, 0, g_head[l][None, :], r_col, dv)
        ogla_c = _gla_post(of_c, ob_c, z, lat_rows, g_head[l][None, :], r_col, dv)
        m = _merge(four_l, four_c, ogla_l, ogla_c, w_branch_a, w_branch_b, l, gl)
        xs = _out_residual(m, w_out, l, xs, mod, 2, n_lat, bsz)
        wr_t = w_router[l].T.astype(BF16)
        h2, aff_t = _norm_mod_router(xs, g_norm2[l][None, :], mod, 4, 3, wr_t, n_lat, bsz)
        xs = _moe(xs, h2, aff_t, mod, 5, w_exp_gate, w_exp_up, w_exp_down, l,
                  0, n_lat, bsz, None)
        if not last:
            xs = _moe(xs, h2, aff_t, mod, 5, w_exp_gate, w_exp_up, w_exp_down, l,
                      lat_rows, n_ctx, bsz, bsz)
    return _final_norm(xs, g_final[None, :], lat_rows).reshape(bsz, n_lat, d)
```

```python
import functools
import math

import jax
import jax.numpy as jnp
from jax import lax
from jax.experimental import pallas as pl
from jax.experimental.pallas import tpu as pltpu

F32 = jnp.float32
BF16 = jnp.bfloat16

N_FOURIER_GROUPS = 4
GLA_HEADS = 4
GATE_TAU = 16.0
CHUNK = 64
N_DIR = 2
CAPACITY_FACTOR = 2
N_MOD = 6
EPS = 1e-6

LANES = 128
DFT_INNER = 128
DFT_UNROLL = 8
VMEM_LIMIT = 56 << 20
GLA_ROWS = 256
DMA_ROWS = 128
DMA_UNROLL = 8


def _cparams(n_axes):
    return pltpu.CompilerParams(dimension_semantics=("arbitrary",) * n_axes,
                                vmem_limit_bytes=VMEM_LIMIT)


def _sigmoid(v):
    return 1.0 / (1.0 + jnp.exp(-v))


def _ada_kernel(c_ref, w_ref, b_ref, o_ref):
    c = c_ref[...]
    a = (c * _sigmoid(c)).astype(BF16)
    o_ref[...] = jnp.dot(a, w_ref[...].astype(BF16), preferred_element_type=F32) + b_ref[...]


def _adaln(c_rows, w_ada, b_ada, tn=512):
    depth, d, n = w_ada.shape
    r = c_rows.shape[0]
    return pl.pallas_call(
        _ada_kernel,
        grid=(depth, n // tn),
        in_specs=[pl.BlockSpec((r, d), lambda l, j: (0, 0)),
                  pl.BlockSpec((None, d, tn), lambda l, j: (l, 0, j)),
                  pl.BlockSpec((None, 1, tn), lambda l, j: (l, 0, j))],
        out_specs=pl.BlockSpec((None, r, tn), lambda l, j: (l, 0, j)),
        out_shape=jax.ShapeDtypeStruct((depth, r, n), F32),
        compiler_params=_cparams(2),
        name="adaln",
    )(c_rows, w_ada, b_ada.reshape(depth, 1, n))


def _normed(x_ref, g_ref, sc_ref, sh_ref):
    x = x_ref[...]
    y = x * lax.rsqrt(jnp.mean(x * x, axis=-1, keepdims=True) + EPS) * g_ref[...]
    return y * (1.0 + sc_ref[...]) + sh_ref[...]


def _norm_kernel(x_ref, g_ref, sc_ref, sh_ref, o_ref):
    o_ref[...] = _normed(x_ref, g_ref, sc_ref, sh_ref).astype(o_ref.dtype)


def _norm_router_kernel(x_ref, g_ref, sc_ref, sh_ref, wr_ref, o_ref, aff_ref):
    h = _normed(x_ref, g_ref, sc_ref, sh_ref)
    o_ref[...] = h
    logits = lax.dot_general(wr_ref[...], h.astype(BF16), (((1,), (1,)), ((), ())),
                             preferred_element_type=F32)
    e = jnp.exp(logits - jnp.max(logits, axis=0, keepdims=True))
    aff_ref[...] = e / jnp.sum(e, axis=0, keepdims=True)


def _seg_of(i, tm, n_lat, n_seg):
    return jnp.minimum((i * tm) // n_lat, n_seg)


def _norm_mod(x, g, mod, k_scale, k_shift, n_lat, n_seg, out_dtype, tm=256):
    t, d = x.shape
    seg = functools.partial(_seg_of, tm=tm, n_lat=n_lat, n_seg=n_seg)
    return pl.pallas_call(
        _norm_kernel,
        grid=(t // tm,),
        in_specs=[pl.BlockSpec((tm, d), lambda i: (i, 0)),
                  pl.BlockSpec((1, d), lambda i: (0, 0)),
                  pl.BlockSpec((None, 1, d), lambda i: (seg(i) * N_MOD + k_scale, 0, 0)),
                  pl.BlockSpec((None, 1, d), lambda i: (seg(i) * N_MOD + k_shift, 0, 0))],
        out_specs=pl.BlockSpec((tm, d), lambda i: (i, 0)),
        out_shape=jax.ShapeDtypeStruct((t, d), out_dtype),
        compiler_params=_cparams(1),
        name="norm_mod",
    )(x, g, mod, mod)


def _norm_mod_router(x, g, mod, k_scale, k_shift, wr_t, n_lat, n_seg, tm=256):
    t, d = x.shape
    e = wr_t.shape[0]
    seg = functools.partial(_seg_of, tm=tm, n_lat=n_lat, n_seg=n_seg)
    return pl.pallas_call(
        _norm_router_kernel,
        grid=(t // tm,),
        in_specs=[pl.BlockSpec((tm, d), lambda i: (i, 0)),
                  pl.BlockSpec((1, d), lambda i: (0, 0)),
                  pl.BlockSpec((None, 1, d), lambda i: (seg(i) * N_MOD + k_scale, 0, 0)),
                  pl.BlockSpec((None, 1, d), lambda i: (seg(i) * N_MOD + k_shift, 0, 0)),
                  pl.BlockSpec((e, d), lambda i: (0, 0))],
        out_specs=[pl.BlockSpec((tm, d), lambda i: (i, 0)),
                   pl.BlockSpec((e, tm), lambda i: (0, i))],
        out_shape=[jax.ShapeDtypeStruct((t, d), F32),
                   jax.ShapeDtypeStruct((e, t), F32)],
        compiler_params=_cparams(1),
        name="norm_mod_router",
    )(x, g, mod, mod, wr_t)


def _final_norm_kernel(x_ref, g_ref, o_ref):
    x = x_ref[...]
    o_ref[...] = x * lax.rsqrt(jnp.mean(x * x, axis=-1, keepdims=True) + EPS) * g_ref[...]


def _final_norm(x, g, rows, tm=256):
    d = x.shape[1]
    return pl.pallas_call(
        _final_norm_kernel,
        grid=(rows // tm,),
        in_specs=[pl.BlockSpec((tm, d), lambda i: (i, 0)),
                  pl.BlockSpec((1, d), lambda i: (0, 0))],
        out_specs=pl.BlockSpec((tm, d), lambda i: (i, 0)),
        out_shape=jax.ShapeDtypeStruct((rows, d), F32),
        compiler_params=_cparams(1),
        name="final_norm",
    )(x, g)


def _cache_weight(w_ref, ws_ref, first):
    @pl.when(first)
    def _():
        ws_ref[...] = w_ref[...].astype(BF16)


def _mm_t_kernel(a_ref, bt_ref, o_ref, bs_ref):
    @pl.when(pl.program_id(1) == 0)
    def _():
        bs_ref[...] = bt_ref[0].T.astype(BF16)

    o_ref[...] = jnp.dot(a_ref[...], bs_ref[...], preferred_element_type=F32).astype(o_ref.dtype)


def _layer_spec(block, index_map, w, l):
    if w.ndim == len(block):
        return pl.BlockSpec(block, index_map)
    return pl.BlockSpec((None,) + block, lambda *g: (l,) + index_map(*g))


def _matmul_t(a, bt, l, col0, ncols, out_dtype=F32, tm=512, tn=512):
    t, k = a.shape
    tn = min(tn, ncols)
    assert col0 % 8 == 0 and t % tm == 0 and ncols % tn == 0
    return pl.pallas_call(
        _mm_t_kernel,
        grid=(ncols // tn, t // tm),
        in_specs=[pl.BlockSpec((tm, k), lambda j, i: (i, 0)),
                  pl.BlockSpec((pl.Element(1), pl.Element(tn), pl.Element(k)),
                               lambda j, i: (l, pl.multiple_of(col0 + j * tn, 8), 0))],
        out_specs=pl.BlockSpec((tm, tn), lambda j, i: (i, j)),
        out_shape=jax.ShapeDtypeStruct((t, ncols), out_dtype),
        scratch_shapes=[pltpu.VMEM((k, tn), BF16)],
        compiler_params=_cparams(2),
        name="matmul_t",
    )(a, bt)


def _merge_kernel(fl_ref, fc_ref, ol_ref, oc_ref, wa_ref, wb_ref, ga_ref, gb_ref, o_ref,
                  was_ref, wbs_ref, *, lat_tiles):
    i = pl.program_id(1)
    _cache_weight(wa_ref, was_ref, i == 0)
    _cache_weight(wb_ref, wbs_ref, i == 0)

    def compute(fa_ref, oa_ref):
        ya = jnp.dot(fa_ref[...], was_ref[...], preferred_element_type=F32)
        yb = jnp.dot(oa_ref[...], wbs_ref[...], preferred_element_type=F32)
        o_ref[...] = (_sigmoid(ga_ref[...]) * ya + _sigmoid(gb_ref[...]) * yb).astype(o_ref.dtype)

    @pl.when(i < lat_tiles)
    def _():
        compute(fl_ref, ol_ref)

    @pl.when(i >= lat_tiles)
    def _():
        compute(fc_ref, oc_ref)


def _merge(four_lat, four_ctx, ogla_lat, ogla_ctx, w_a, w_b, l, gates, tm=512, tn=512):
    ka, kb = four_lat.shape[1], ogla_lat.shape[1]
    lat_tiles = four_lat.shape[0] // tm
    t = gates.shape[0]
    d = w_a.shape[-1]
    nj = d // tn

    def lat(j, i):
        return (jnp.minimum(i, lat_tiles - 1), 0)

    def ctx(j, i):
        return (jnp.maximum(i - lat_tiles, 0), 0)

    return pl.pallas_call(
        functools.partial(_merge_kernel, lat_tiles=lat_tiles),
        grid=(nj, t // tm),
        in_specs=[pl.BlockSpec((tm, ka), lat),
                  pl.BlockSpec((tm, ka), ctx),
                  pl.BlockSpec((tm, kb), lat),
                  pl.BlockSpec((tm, kb), ctx),
                  _layer_spec((ka, tn), lambda j, i: (0, j), w_a, l),
                  _layer_spec((kb, tn), lambda j, i: (0, j), w_b, l),
                  pl.BlockSpec((tm, tn), lambda j, i: (i, j)),
                  pl.BlockSpec((tm, tn), lambda j, i: (i, nj + j))],
        out_specs=pl.BlockSpec((tm, tn), lambda j, i: (i, j)),
        out_shape=jax.ShapeDtypeStruct((t, d), BF16),
        scratch_shapes=[pltpu.VMEM((ka, tn), BF16), pltpu.VMEM((kb, tn), BF16)],
        compiler_params=_cparams(2),
        name="merge",
    )(four_lat, four_ctx, ogla_lat, ogla_ctx, w_a, w_b, gates, gates)


def _out_res_kernel(a_ref, b_ref, x_ref, gt_ref, o_ref, bs_ref):
    _cache_weight(b_ref, bs_ref, pl.program_id(1) == 0)
    y = jnp.dot(a_ref[...], bs_ref[...], preferred_element_type=F32)
    o_ref[...] = x_ref[...] + gt_ref[...] * y


def _out_residual(m, w_out, l, x, mod, k_gate, n_lat, n_seg, tm=512, tn=512):
    t, k = m.shape
    d = w_out.shape[-1]
    seg = functools.partial(_seg_of, tm=tm, n_lat=n_lat, n_seg=n_seg)
    return pl.pallas_call(
        _out_res_kernel,
        grid=(d // tn, t // tm),
        in_specs=[pl.BlockSpec((tm, k), lambda j, i: (i, 0)),
                  _layer_spec((k, tn), lambda j, i: (0, j), w_out, l),
                  pl.BlockSpec((tm, tn), lambda j, i: (i, j)),
                  pl.BlockSpec((None, 1, tn), lambda j, i: (seg(i) * N_MOD + k_gate, 0, j))],
        out_specs=pl.BlockSpec((tm, tn), lambda j, i: (i, j)),
        out_shape=jax.ShapeDtypeStruct((t, d), F32),
        scratch_shapes=[pltpu.VMEM((k, tn), BF16)],
        compiler_params=_cparams(2),
        name="out_residual",
    )(m, w_out, x, mod)


def _chan_dft_kernel(u_ref, w_ref, p_ref, q_ref):
    c = p_ref.shape[1]
    y = jnp.dot(u_ref[...].astype(BF16), w_ref[...], preferred_element_type=F32)
    p_ref[...] = y[:, :c]
    q_ref[...] = y[:, c:]


def _chan_dft(z, wc, width, tm=512):
    t = z.shape[0]
    c = wc.shape[0]
    return pl.pallas_call(
        _chan_dft_kernel,
        grid=(t // tm, width // c),
        in_specs=[pl.BlockSpec((tm, c), lambda i, g: (i, g)),
                  pl.BlockSpec((c, 2 * c), lambda i, g: (0, 0))],
        out_specs=[pl.BlockSpec((tm, c), lambda i, g: (i, g)),
                   pl.BlockSpec((tm, c), lambda i, g: (i, g))],
        out_shape=[jax.ShapeDtypeStruct((t, width), F32)] * 2,
        compiler_params=_cparams(2),
        name="chan_dft",
    )(z, wc)


def _seq_dft_kernel(p_ref, q_ref, m1_ref, m2_ref, o_ref, bs_ref, os_ref, *, n1, scale):
    n2 = DFT_INNER

    def stage1(i, carry):
        xp = p_ref[pl.ds(i, n2, stride=n1), :]
        xq = q_ref[pl.ds(i, n2, stride=n1), :]
        xs = jnp.concatenate([xp, xq], axis=0).astype(BF16)
        row0 = pl.multiple_of(i * (2 * n2), 2 * n2)
        bs_ref[pl.ds(row0, 2 * n2), :] = jnp.dot(m1_ref[i], xs, preferred_element_type=F32)
        return carry

    lax.fori_loop(0, n1, stage1, 0, unroll=DFT_UNROLL)

    def stage2(k2, carry):
        are = bs_ref[pl.ds(k2, n1, stride=2 * n2), :]
        aim = bs_ref[pl.ds(n2 + k2, n1, stride=2 * n2), :]
        a = jnp.concatenate([are, aim], axis=0).astype(BF16)
        y = jnp.dot(m2_ref[...], a, preferred_element_type=F32) * scale
        os_ref[pl.ds(k2, n1, stride=n2), :] = y
        return carry

    lax.fori_loop(0, n2, stage2, 0, unroll=DFT_UNROLL)
    o_ref[...] = os_ref[...].astype(o_ref.dtype)


def _seq_dft(p, q, m1, m2, n_seq, n_seg, scale, ft=128):
    t, w = p.shape
    n1 = n_seq // DFT_INNER
    kern = functools.partial(_seq_dft_kernel, n1=n1, scale=scale)
    return pl.pallas_call(
        kern,
        grid=(n_seg, w // ft),
        in_specs=[pl.BlockSpec((n_seq, ft), lambda b, j: (b, j)),
                  pl.BlockSpec((n_seq, ft), lambda b, j: (b, j)),
                  pl.BlockSpec(memory_space=pltpu.VMEM),
                  pl.BlockSpec(memory_space=pltpu.VMEM)],
        out_specs=pl.BlockSpec((n_seq, ft), lambda b, j: (b, j)),
        out_shape=jax.ShapeDtypeStruct((n_seg * n_seq, w), BF16),
        scratch_shapes=[pltpu.VMEM((n1 * 2 * DFT_INNER, ft), F32), pltpu.VMEM((n_seq, ft), F32)],
        compiler_params=_cparams(2),
        name="seq_dft",
    )(p, q, m1, m2)


def _ctx_dft_kernel(p_ref, q_ref, m_ref, o_ref, *, scale):
    a = jnp.concatenate([p_ref[...], q_ref[...]], axis=0).astype(BF16)
    o_ref[...] = (jnp.dot(m_ref[...], a, preferred_element_type=F32) * scale).astype(o_ref.dtype)


def _ctx_dft(p, q, mc, row_blk0, n_ctx, n_seg, scale, ft=512):
    w = p.shape[1]
    kern = functools.partial(_ctx_dft_kernel, scale=scale)
    spec = pl.BlockSpec((n_ctx, ft), lambda b, j: (row_blk0 + b, j))
    return pl.pallas_call(
        kern,
        grid=(n_seg, w // ft),
        in_specs=[spec, spec, pl.BlockSpec((n_ctx, 2 * n_ctx), lambda b, j: (0, 0))],
        out_specs=pl.BlockSpec((n_ctx, ft), lambda b, j: (b, j)),
        out_shape=jax.ShapeDtypeStruct((n_seg * n_ctx, w), BF16),
        compiler_params=_cparams(2),
        name="ctx_dft",
    )(p, q, mc)


def _dft_tables(n_seq, n_ctx, c):
    def cs(num, den):
        ang = (2.0 * math.pi / den) * (num % den).astype(F32)
        return jnp.cos(ang), jnp.sin(ang)

    ic = jnp.arange(c, dtype=jnp.int32)
    cc, sc = cs(ic[:, None] * ic[None, :], c)
    wc = jnp.concatenate([cc, -sc], axis=1).astype(BF16)

    n2 = DFT_INNER
    n1 = n_seq // n2
    i1 = jnp.arange(n1, dtype=jnp.int32)[:, None, None]
    k2 = jnp.arange(n2, dtype=jnp.int32)[None, :, None]
    j2 = jnp.arange(n2, dtype=jnp.int32)[None, None, :]
    c1, s1 = cs((i1 + n1 * j2) * k2, n_seq)
    m1 = jnp.concatenate([jnp.concatenate([c1, s1], axis=2),
                          jnp.concatenate([-s1, c1], axis=2)], axis=1).astype(BF16)
    k1 = jnp.arange(n1, dtype=jnp.int32)
    c2, s2 = cs(k1[:, None] * k1[None, :], n1)
    m2 = jnp.concatenate([c2, s2], axis=1).astype(BF16)

    il = jnp.arange(n_ctx, dtype=jnp.int32)
    cl, sl = cs(il[:, None] * il[None, :], n_ctx)
    mc = jnp.concatenate([cl, sl], axis=1).astype(BF16)
    return wc, m1, m2, mc


def _exact_tri_matmul(tri, v):
    h1 = v.astype(BF16)
    r1 = v - h1.astype(F32)
    h2 = r1.astype(BF16)
    h3 = (r1 - h2.astype(F32)).astype(BF16)
    return (jnp.dot(tri, h1, preferred_element_type=F32)
            + jnp.dot(tri, h2, preferred_element_type=F32)
            + jnp.dot(tri, h3, preferred_element_type=F32))


def _gla_kernel(qf_ref, kf_ref, vf_ref, lrf_ref, qb_ref, kb_ref, vb_ref, lrb_ref, wg_ref, bg_ref,
                s0_ref, of_ref, ob_ref, sout_ref, st_ref, *, rank, q_scale):
    t = pl.program_id(2)
    rows = qf_ref.shape[0]
    n_chunks = rows // CHUNK

    @pl.when(t == 0)
    def _():
        st_ref[...] = s0_ref[...]

    ii = lax.broadcasted_iota(jnp.int32, (rows, rows), 0)
    jj = lax.broadcasted_iota(jnp.int32, (rows, rows), 1)
    same_chunk = (ii // CHUNK) == (jj // CHUNK)
    dirs = ((qf_ref, kf_ref, vf_ref, lrf_ref, of_ref, same_chunk & (jj <= ii)),
            (qb_ref, kb_ref, vb_ref, lrb_ref, ob_ref, same_chunk & (jj >= ii)))

    pre = []
    for d, (q_ref, k_ref, v_ref, lr_ref, _, keep) in enumerate(dirs):
        lr = lr_ref[:, d * rank:(d + 1) * rank].astype(BF16)
        gl = jnp.dot(lr, wg_ref[d].astype(BF16), preferred_element_type=F32) + bg_ref[d]
        logg = (jnp.minimum(gl, 0.0) - jnp.log1p(jnp.exp(-jnp.abs(gl)))) * (1.0 / GATE_TAU)
        tri = jnp.where(keep, 1.0, 0.0).astype(BF16)
        cum = _exact_tri_matmul(tri, logg)
        tots = [cum[j * CHUNK:j * CHUNK + 1, :] if d else cum[(j + 1) * CHUNK - 1:(j + 1) * CHUNK, :]
                for j in range(n_chunks)]
        tot_rows = jnp.concatenate([jnp.broadcast_to(t_, (CHUNK, t_.shape[1])) for t_ in tots], axis=0)
        q = q_ref[...] * q_scale
        k = k_ref[...]
        vb = v_ref[...].astype(BF16)
        qd = (q * jnp.exp(cum)).astype(BF16)
        ki = (k * jnp.exp(-cum)).astype(BF16)
        kt = (k * jnp.exp(tot_rows - cum)).astype(BF16)
        att = lax.dot_general(qd, ki, (((1,), (1,)), ((), ())), preferred_element_type=F32)
        att = jnp.where(keep, att, 0.0).astype(BF16)
        o_intra = jnp.dot(att, vb, preferred_element_type=F32)
        pre.append((qd, kt, tots, o_intra))

    def chunk(d, j):
        v_ref, o_ref = dirs[d][2], dirs[d][4]
        qd, kt, tots, o_intra = pre[d]
        rs = slice(j * CHUNK, (j + 1) * CHUNK)
        st = st_ref[d]
        o_inter = lax.dot_general(qd[rs, :], st.astype(BF16), (((1,), (1,)), ((), ())),
                                  preferred_element_type=F32)
        o_ref[rs, :] = o_intra[rs, :] + o_inter
        upd = jnp.dot(v_ref[rs, :].T.astype(BF16), kt[rs, :], preferred_element_type=F32)
        st_ref[d] = st * jnp.exp(tots[j]) + upd

    for j in range(n_chunks):
        chunk(0, j)
        chunk(1, n_chunks - 1 - j)

    @pl.when(t == pl.num_programs(2) - 1)
    def _():
        sout_ref[...] = st_ref[...]


def _gla(z, lr, w_gate2, b_gate, s0, row0, seg_len, n_seg, q_col, k_col, v_col, dk, dv):
    heads = GLA_HEADS
    rank = w_gate2.shape[1]
    rows = GLA_ROWS
    nt = seg_len // rows
    blk0 = row0 // rows

    def fwd(b, s):
        return b * nt + s

    def bwd(b, s):
        return b * nt + nt - 1 - s

    def zspec(width, col, blk):
        return pl.BlockSpec((rows, width), lambda b, h, s: (blk0 + blk(b, s), col // width + h))

    def lrspec(blk):
        return pl.BlockSpec((rows, lr.shape[1]), lambda b, h, s: (blk0 + blk(b, s), 0))

    state = pl.BlockSpec((None, None, N_DIR, dv, dk), lambda b, h, s: (b, h, 0, 0, 0))
    kern = functools.partial(_gla_kernel, rank=rank, q_scale=dk ** -0.5)
    return pl.pallas_call(
        kern,
        grid=(n_seg, heads, nt),
        in_specs=[zspec(dk, q_col, fwd), zspec(dk, k_col, fwd), zspec(dv, v_col, fwd), lrspec(fwd),
                  zspec(dk, q_col, bwd), zspec(dk, k_col, bwd), zspec(dv, v_col, bwd), lrspec(bwd),
                  pl.BlockSpec((N_DIR, rank, dk), lambda b, h, s: (0, 0, h)),
                  pl.BlockSpec((N_DIR, 1, dk), lambda b, h, s: (0, 0, h)),
                  state],
        out_specs=[pl.BlockSpec((rows, dv), lambda b, h, s: (fwd(b, s), h)),
                   pl.BlockSpec((rows, dv), lambda b, h, s: (bwd(b, s), h)),
                   state],
        out_shape=[jax.ShapeDtypeStruct((n_seg * seg_len, heads * dv), F32),
                   jax.ShapeDtypeStruct((n_seg * seg_len, heads * dv), F32),
                   jax.ShapeDtypeStruct((n_seg, heads, N_DIR, dv, dk), F32)],
        scratch_shapes=[pltpu.VMEM((N_DIR, dv, dk), F32)],
        compiler_params=_cparams(3),
        name="gla_scan",
    )(z, z, z, lr, z, z, z, lr, w_gate2, b_gate.reshape(N_DIR, 1, -1), s0)


def _gla_post_kernel(of_ref, ob_ref, r_ref, g_ref, o_ref, *, dv):
    o = of_ref[...] + ob_ref[...]
    r = r_ref[...]
    gate = g_ref[...] * (r * _sigmoid(r))
    for h in range(o.shape[1] // dv):
        cs = slice(h * dv, (h + 1) * dv)
        oh = o[:, cs]
        oh = oh * lax.rsqrt(jnp.mean(oh * oh, axis=-1, keepdims=True) + EPS)
        o_ref[:, cs] = (oh * gate[:, cs]).astype(o_ref.dtype)


def _gla_post(o_f, o_b, z, row0, g_head, r_col, dv, tm=256):
    t, gv = o_f.shape
    blk0 = row0 // tm
    kern = functools.partial(_gla_post_kernel, dv=dv)
    return pl.pallas_call(
        kern,
        grid=(t // tm,),
        in_specs=[pl.BlockSpec((tm, gv), lambda i: (i, 0)),
                  pl.BlockSpec((tm, gv), lambda i: (i, 0)),
                  pl.BlockSpec((tm, gv), lambda i: (blk0 + i, r_col // gv)),
                  pl.BlockSpec((1, gv), lambda i: (0, 0))],
        out_specs=pl.BlockSpec((tm, gv), lambda i: (i, 0)),
        out_shape=jax.ShapeDtypeStruct((t, gv), BF16),
        compiler_params=_cparams(1),
        name="gla_post",
    )(o_f, o_b, z, g_head)


_ROUTE_COLS = 16


def _route_kernel(aff_ref, idx_ref, gate_ref, c_ref, m_ref, row_ref, res_ref, win_ref, w0_ref,
                  *, cap):
    e = pl.program_id(1)
    n_exp, n = aff_ref.shape
    lanes = 128
    nblk = n // lanes

    @pl.when(e == 0)
    def _():
        a = aff_ref[...]
        bits = pltpu.bitcast(a, jnp.int32)
        thr = jnp.zeros((n_exp, 1), jnp.int32)
        def count(pred):
            return jnp.sum(jnp.where(pred, 1.0, 0.0), axis=1, keepdims=True)

        for b in range(30, -1, -1):
            cand = thr | (1 << b)
            thr = jnp.where(count(bits >= cand) >= cap, cand, thr)
        gt = bits > thr
        eq = bits == thr
        need = cap - count(gt)
        pos = lax.broadcasted_iota(jnp.int32, (n_exp, n), 1)
        bound = jnp.zeros((n_exp, 1), jnp.int32)
        for b in range(int(math.log2(n)), -1, -1):
            cand = bound + (1 << b)
            bound = jnp.where(count(eq & (pos < cand)) <= need, cand, bound)
        sel = gt | (eq & (pos < bound))
        m = jnp.where(sel, 1.0, 0.0)
        m_ref[...] = m
        ri = lax.broadcasted_iota(jnp.int32, (lanes, lanes), 0)
        ci = lax.broadcasted_iota(jnp.int32, (lanes, lanes), 1)
        tri = jnp.where(ri <= ci, 1.0, 0.0).astype(BF16)
        off = jnp.zeros((n_exp, 1), F32)
        for j in range(nblk):
            cj = jnp.dot(m[:, j * lanes:(j + 1) * lanes].astype(BF16), tri,
                         preferred_element_type=F32) + off
            c_ref[:, j * lanes:(j + 1) * lanes] = cj
            off = cj[:, lanes - 1:lanes]

    pick = lax.broadcasted_iota(jnp.int32, (n_exp, n), 0) == e
    for k, src in enumerate((c_ref, m_ref, aff_ref)):
        row_ref[k:k + 1, :] = jnp.sum(jnp.where(pick, src[...], 0.0), axis=0, keepdims=True)

    win = lanes + 8
    res_ref[...] = jnp.zeros_like(res_ref)
    lane_row = lax.broadcasted_iota(jnp.int32, (1, lanes), 1).astype(F32).astype(BF16)
    pad_rows = jnp.zeros((_ROUTE_COLS - 5, lanes), BF16)
    row_in_win = lax.broadcasted_iota(jnp.int32, (win, lanes), 0)

    def match(j, carry):
        col = pl.multiple_of(j * lanes, lanes)
        cj = row_ref[0:1, pl.ds(col, lanes)]
        mj = row_ref[1:2, pl.ds(col, lanes)]
        aj = row_ref[2:3, pl.ds(col, lanes)]
        before = (cj[0, 0] - mj[0, 0]).astype(jnp.int32)
        w0 = (before // 8) * 8
        slot1 = (row_in_win + (w0 + 1)).astype(F32)
        onehot = jnp.where((cj == slot1) & (mj > 0.5), 1.0, 0.0).astype(BF16)
        a1 = aj.astype(BF16)
        r1 = aj - a1.astype(F32)
        a2 = r1.astype(BF16)
        a3 = (r1 - a2.astype(F32)).astype(BF16)
        blk_row = jnp.full((1, lanes), j, F32).astype(BF16)
        vals = jnp.concatenate([lane_row, blk_row, a1, a2, a3, pad_rows], axis=0)
        y = lax.dot_general(onehot, vals, (((1,), (1,)), ((), ())), preferred_element_type=F32)
        win_ref[pl.ds(pl.multiple_of(j * win, 8), win), 0:_ROUTE_COLS] = y
        w0_ref[j] = w0
        return carry

    def place(j, carry):
        w0 = pl.multiple_of(w0_ref[j], 8)
        y = win_ref[pl.ds(pl.multiple_of(j * win, 8), win), 0:_ROUTE_COLS]
        res_ref[pl.ds(w0, win), 0:_ROUTE_COLS] = res_ref[pl.ds(w0, win), 0:_ROUTE_COLS] + y
        return carry

    lax.fori_loop(0, nblk, match, 0, unroll=min(4, nblk))
    lax.fori_loop(0, nblk, place, 0, unroll=2)
    res = res_ref[0:cap, :]
    lane = lax.broadcasted_iota(jnp.int32, res.shape, 1)
    tok = jnp.where(lane == 0, res, 0.0) + jnp.where(lane == 1, res * lanes, 0.0)
    idx_ref[...] = jnp.sum(tok, axis=1, keepdims=True).astype(jnp.int32)
    gate_ref[...] = jnp.sum(jnp.where((lane >= 2) & (lane <= 4), res, 0.0), axis=1, keepdims=True)


def _route(aff_t, col_blk0, n_set, n_sets, cap):
    n_exp = aff_t.shape[0]
    kern = functools.partial(_route_kernel, cap=cap)
    return pl.pallas_call(
        kern,
        grid=(n_sets, n_exp),
        in_specs=[pl.BlockSpec((n_exp, n_set), lambda s, e: (0, col_blk0 + s))],
        out_specs=[pl.BlockSpec((None, None, cap, 1), lambda s, e: (s, e, 0, 0)),
                   pl.BlockSpec((None, None, cap, 1), lambda s, e: (s, e, 0, 0))],
        out_shape=[jax.ShapeDtypeStruct((n_sets, n_exp, cap, 1), jnp.int32),
                   jax.ShapeDtypeStruct((n_sets, n_exp, cap, 1), F32)],
        scratch_shapes=[pltpu.VMEM((n_exp, n_set), F32), pltpu.VMEM((n_exp, n_set), F32),
                        pltpu.VMEM((8, n_set), F32), pltpu.VMEM((cap + 128 + 8, 128), F32),
                        pltpu.VMEM((n_set // 128 * (128 + 8), 128), F32),
                        pltpu.SMEM((n_set // 128,), jnp.int32)],
        compiler_params=_cparams(2),
        name="route",
    )(aff_t)


def _row_copy(src, src_row, dst, dst_row, sem):
    return pltpu.make_async_copy(src.at[pl.ds(src_row, 1), :], dst.at[pl.ds(dst_row, 1), :], sem)


def _gather_kernel(idx_ref, h_hbm, o_ref, buf, sem):
    n = buf.shape[0]
    base = pl.program_id(0) * n

    def start(r, c):
        _row_copy(h_hbm, idx_ref[base + r], buf, r, sem).start()
        return c

    def wait(r, c):
        _row_copy(h_hbm, idx_ref[base + r], buf, r, sem).wait()
        return c

    lax.fori_loop(0, n, start, 0, unroll=DMA_UNROLL)
    lax.fori_loop(0, n, wait, 0, unroll=DMA_UNROLL)
    o_ref[...] = buf[...].astype(o_ref.dtype)


def _gather_rows(idx, h, n=DMA_ROWS):
    r = idx.shape[0]
    n = min(n, r)
    d = h.shape[1]
    return pl.pallas_call(
        _gather_kernel,
        grid_spec=pltpu.PrefetchScalarGridSpec(
            num_scalar_prefetch=1,
            grid=(r // n,),
            in_specs=[pl.BlockSpec(memory_space=pl.ANY)],
            out_specs=pl.BlockSpec((n, d), lambda g, idx_ref: (g, 0)),
            scratch_shapes=[pltpu.VMEM((n, d), F32), pltpu.SemaphoreType.DMA(())]),
        out_shape=jax.ShapeDtypeStruct((r, d), BF16),
        compiler_params=_cparams(1),
        name="gather_rows",
    )(idx, h)


def _down_scatter_kernel(idx_ref, hid_ref, gate_ref, gt_ref, wd_hbm, x_in, x_hbm,
                         wf_ref, wb_ref, y_ref, xb_ref, wsem, rsem, osem, *, layer, tn, serial):
    del x_in
    e, i = pl.program_id(0), pl.program_id(1)
    n_e, n_i = pl.num_programs(0), pl.num_programs(1)
    n, d = y_ref.shape
    k = e * n_i + i
    slot = k % 2
    base = k * n

    def weight_copy(ee):
        return pltpu.make_async_copy(wd_hbm.at[layer, ee], wf_ref, wsem)

    def fetch(r, kk, sl):
        return _row_copy(x_hbm, idx_ref[kk * n + r], xb_ref.at[sl], r, rsem.at[sl])

    def put(r, kk, sl):
        return _row_copy(xb_ref.at[sl], r, x_hbm, idx_ref[kk * n + r], osem.at[sl])

    def each_row(fn):
        lax.fori_loop(0, n, lambda r, c: (fn(r), c)[1], 0, unroll=DMA_UNROLL)

    @pl.when(i == 0)
    def _():
        @pl.when(e == 0)
        def _():
            weight_copy(e).start()

        weight_copy(e).wait()
        wb_ref[...] = wf_ref[...].astype(BF16)

        @pl.when(e + 1 < n_e)
        def _():
            weight_copy(e + 1).start()

    if serial:
        @pl.when(k > 0)
        def _():
            each_row(lambda r: put(r, k - 1, 1 - slot).wait())

    each_row(lambda r: fetch(r, k, slot).start())
    hid = hid_ref[...]
    for p in range(d // tn):
        cs = slice(p * tn, (p + 1) * tn)
        y_ref[:, cs] = jnp.dot(hid, wb_ref[:, cs], preferred_element_type=F32) * gate_ref[...]
    each_row(lambda r: fetch(r, k, slot).wait())
    xb_ref[slot] = xb_ref[slot] + gt_ref[...] * y_ref[...]
    each_row(lambda r: put(r, k, slot).start())

    if not serial:
        @pl.when(k > 0)
        def _():
            each_row(lambda r: put(r, k - 1, 1 - slot).wait())

    @pl.when(k == n_e * n_i - 1)
    def _():
        each_row(lambda r: put(r, k, slot).wait())


def _expert_down_scatter(idx, hid, w_d, l, gates, x, mod, k_gate, rows_per_set, n_sets, mod_row,
                         tr=512, tn=1024):
    n_exp, r, ff = hid.shape
    d = w_d.shape[-1]
    tr = min(tr, rows_per_set)
    tn = min(tn, d)
    tiles_per_set = rows_per_set // tr
    assert rows_per_set % tr == 0 and r == n_sets * rows_per_set

    def gate_blk(e, i, idx_ref):
        row = (i // tiles_per_set) % n_sets if mod_row is None else mod_row
        return (row * N_MOD + k_gate, 0, 0)

    kern = functools.partial(_down_scatter_kernel, layer=l, tn=tn, serial=n_sets < 2)
    return pl.pallas_call(
        kern,
        grid_spec=pltpu.PrefetchScalarGridSpec(
            num_scalar_prefetch=1,
            grid=(n_exp, r // tr),
            in_specs=[pl.BlockSpec((None, tr, ff), lambda e, i, idx_ref: (e, i, 0)),
                      pl.BlockSpec((None, tr, 1), lambda e, i, idx_ref: (e, i, 0)),
                      pl.BlockSpec((None, 1, d), gate_blk),
                      pl.BlockSpec(memory_space=pl.ANY),
                      pl.BlockSpec(memory_space=pl.ANY)],
            out_specs=pl.BlockSpec(memory_space=pl.ANY),
            scratch_shapes=[pltpu.VMEM((ff, d), F32), pltpu.VMEM((ff, d), BF16),
                            pltpu.VMEM((tr, d), F32), pltpu.VMEM((2, tr, d), F32),
                            pltpu.SemaphoreType.DMA(()), pltpu.SemaphoreType.DMA((2,)),
                            pltpu.SemaphoreType.DMA((2,))]),
        out_shape=jax.ShapeDtypeStruct(x.shape, F32),
        input_output_aliases={5: 0},
        compiler_params=_cparams(2),
        name="expert_down_scatter",
    )(idx, hid, gates, mod, w_d, x)


def _expert_up_kernel(x_ref, wg_ref, wu_ref, o_ref, wgs_ref, wus_ref):
    first = pl.program_id(2) == 0
    _cache_weight(wg_ref, wgs_ref, first)
    _cache_weight(wu_ref, wus_ref, first)
    x = x_ref[...]
    g = jnp.dot(x, wgs_ref[...], preferred_element_type=F32)
    u = jnp.dot(x, wus_ref[...], preferred_element_type=F32)
    o_ref[...] = (g * _sigmoid(g) * u).astype(o_ref.dtype)


def _expert_up(xin, w_g, w_u, l, tr=1024, tf=256):
    n_exp, r, d = xin.shape
    ff = w_g.shape[-1]
    tr = min(tr, r)
    wspec = pl.BlockSpec((None, None, d, tf), lambda e, f, i: (l, e, 0, f))
    return pl.pallas_call(
        _expert_up_kernel,
        grid=(n_exp, ff // tf, r // tr),
        in_specs=[pl.BlockSpec((None, tr, d), lambda e, f, i: (e, i, 0)), wspec, wspec],
        out_specs=pl.BlockSpec((None, tr, tf), lambda e, f, i: (e, i, f)),
        out_shape=jax.ShapeDtypeStruct((n_exp, r, ff), BF16),
        scratch_shapes=[pltpu.VMEM((d, tf), BF16), pltpu.VMEM((d, tf), BF16)],
        compiler_params=_cparams(3),
        name="expert_up",
    )(xin, w_g, w_u)


def _moe(x, h2, aff_t, mod, k_gate, w_g, w_u, w_d, l, row0, n_set, n_sets, mod_row):
    n_exp = aff_t.shape[0]
    d = x.shape[1]
    cap = CAPACITY_FACTOR * n_set // n_exp
    idx, gates = _route(aff_t, row0 // n_set, n_set, n_sets, cap)
    offs = row0 + n_set * jnp.arange(n_sets, dtype=jnp.int32)
    rows = (idx.reshape(n_sets, n_exp, cap) + offs[:, None, None]).transpose(1, 0, 2).reshape(-1)
    gates = gates.reshape(n_sets, n_exp, cap).transpose(1, 0, 2).reshape(n_exp, n_sets * cap, 1)
    xin = _gather_rows(rows, h2).reshape(n_exp, n_sets * cap, d)
    hid = _expert_up(xin, w_g, w_u, l)
    return _expert_down_scatter(rows, hid, w_d, l, gates, x, mod, k_gate, cap, n_sets, mod_row)


def kernel(x, c, ctx, c_ctx, w_ada, b_ada, g_norm1, w_in, w_gate2, b_gate, g_head, w_branch_a,
           w_branch_b, w_out, g_norm2, w_router, w_exp_gate, w_exp_up, w_exp_down, g_final):
    bsz, n_lat, d = x.shape
    n_ctx = ctx.shape[1]
    depth = w_ada.shape[0]
    fw = w_branch_a.shape[1]
    gk = w_gate2.shape[3]
    gv = g_head.shape[1]
    rank = w_gate2.shape[2]
    dk, dv = gk // GLA_HEADS, gv // GLA_HEADS
    cgrp = fw // N_FOURIER_GROUPS
    lat_rows = bsz * n_lat
    q_col, k_col, v_col, r_col = fw, fw + gk, fw + 2 * gk, fw + 2 * gk + gv
    lr_col = r_col + gv
    gl_col = lr_col + N_DIR * rank
    lr_w = 128

    xs = jnp.concatenate([x.reshape(lat_rows, d), ctx.reshape(bsz * n_ctx, d)], axis=0)
    c_rows = jnp.concatenate([c, c_ctx[None, :], jnp.zeros((8 - bsz - 1, d), F32)], axis=0)
    mod_all = _adaln(c_rows, w_ada, b_ada)
    wc, m1, m2, mc = _dft_tables(n_lat, n_ctx, cgrp)
    zero_state = jnp.zeros((bsz, GLA_HEADS, N_DIR, dv, dk), F32)
    w_in_t = jnp.swapaxes(w_in, 1, 2)

    for l in range(depth):
        last = l == depth - 1
        mod = mod_all[l].reshape(8 * N_MOD, 1, d)
        h = _norm_mod(xs, g_norm1[l][None, :], mod, 1, 0, n_lat, bsz, BF16)
        tm_in = 768 if h.shape[0] % 768 == 0 else 512
        z = _matmul_t(h, w_in_t, l, 0, lr_col, tm=tm_in)
        lr = _matmul_t(h, w_in_t, l, lr_col, lr_w)
        gl = _matmul_t(h, w_in_t, l, gl_col, 2 * d, tm=tm_in)
        p, q = _chan_dft(z, wc, fw)
        four_l = _seq_dft(p, q, m1, m2, n_lat, bsz, 1.0 / math.sqrt(n_lat * cgrp))
        four_c = _ctx_dft(p, q, mc, lat_rows // n_ctx, n_ctx, bsz, 1.0 / math.sqrt(n_ctx * cgrp))
        of_c, ob_c, s_c = _gla(z, lr, w_gate2[l], b_gate[l], zero_state, lat_rows, n_ctx, bsz,
                               q_col, k_col, v_col, dk, dv)
        of_l, ob_l, _ = _gla(z, lr, w_gate2[l], b_gate[l], s_c, 0, n_lat, bsz,
                             q_col, k_col, v_col, dk, dv)
        ogla_l = _gla_post(of_l, ob_l, z, 0, g_head[l][None, :], r_col, dv)
        ogla_c = _gla_post(of_c, ob_c, z, lat_rows, g_head[l][None, :], r_col, dv)
        m = _merge(four_l, four_c, ogla_l, ogla_c, w_branch_a, w_branch_b, l, gl)
        xs = _out_residual(m, w_out, l, xs, mod, 2, n_lat, bsz)
        wr_t = w_router[l].T.astype(BF16)
        h2, aff_t = _norm_mod_router(xs, g_norm2[l][None, :], mod, 4, 3, wr_t, n_lat, bsz)
        xs = _moe(xs, h2, aff_t, mod, 5, w_exp_gate, w_exp_up, w_exp_down, l,
                  0, n_lat, bsz, None)
        if not last:
            xs = _moe(xs, h2, aff_t, mod, 5, w_exp_gate, w_exp_up, w_exp_down, l,
                      lat_rows, n_ctx, bsz, bsz)
    return _final_norm(xs, g_final[None, :], lat_rows).reshape(bsz, n_lat, d)
```

```python
import functools
import math

import jax
import jax.numpy as jnp
from jax import lax
from jax.experimental import pallas as pl
from jax.experimental.pallas import tpu as pltpu

F32 = jnp.float32
BF16 = jnp.bfloat16

N_FOURIER_GROUPS = 4
GLA_HEADS = 4
GATE_TAU = 16.0
CHUNK = 64
N_DIR = 2
CAPACITY_FACTOR = 2
N_MOD = 6
EPS = 1e-6

LANES = 128
DFT_INNER = 128
DFT_UNROLL = 8
VMEM_LIMIT = 56 << 20
GLA_ROWS = 256
DMA_ROWS = 128
DMA_UNROLL = 8


def _cparams(n_axes):
    return pltpu.CompilerParams(dimension_semantics=("arbitrary",) * n_axes,
                                vmem_limit_bytes=VMEM_LIMIT)


def _sigmoid(v):
    return 1.0 / (1.0 + jnp.exp(-v))


def _ada_kernel(c_ref, w_ref, b_ref, o_ref):
    c = c_ref[...]
    a = (c * _sigmoid(c)).astype(BF16)
    o_ref[...] = jnp.dot(a, w_ref[...].astype(BF16), preferred_element_type=F32) + b_ref[...]


def _adaln(c_rows, w_ada, b_ada, tn=512):
    depth, d, n = w_ada.shape
    r = c_rows.shape[0]
    return pl.pallas_call(
        _ada_kernel,
        grid=(depth, n // tn),
        in_specs=[pl.BlockSpec((r, d), lambda l, j: (0, 0)),
                  pl.BlockSpec((None, d, tn), lambda l, j: (l, 0, j)),
                  pl.BlockSpec((None, 1, tn), lambda l, j: (l, 0, j))],
        out_specs=pl.BlockSpec((None, r, tn), lambda l, j: (l, 0, j)),
        out_shape=jax.ShapeDtypeStruct((depth, r, n), F32),
        compiler_params=_cparams(2),
        name="adaln",
    )(c_rows, w_ada, b_ada.reshape(depth, 1, n))


def _normed(x_ref, g_ref, sc_ref, sh_ref):
    x = x_ref[...]
    y = x * lax.rsqrt(jnp.mean(x * x, axis=-1, keepdims=True) + EPS) * g_ref[...]
    return y * (1.0 + sc_ref[...]) + sh_ref[...]


def _norm_kernel(x_ref, g_ref, sc_ref, sh_ref, o_ref):
    o_ref[...] = _normed(x_ref, g_ref, sc_ref, sh_ref).astype(o_ref.dtype)


def _norm2_kernel(xl_ref, xc_ref, g_ref, sc_ref, sh_ref, o_ref, *, lat_tiles):
    @pl.when(pl.program_id(0) < lat_tiles)
    def _():
        o_ref[...] = _normed(xl_ref, g_ref, sc_ref, sh_ref).astype(o_ref.dtype)

    @pl.when(pl.program_id(0) >= lat_tiles)
    def _():
        o_ref[...] = _normed(xc_ref, g_ref, sc_ref, sh_ref).astype(o_ref.dtype)


def _split_rows(lat_tiles):
    def lat(i):
        return (jnp.minimum(i, lat_tiles - 1), 0)

    def ctx(i):
        return (jnp.maximum(i - lat_tiles, 0), 0)

    return lat, ctx


def _norm_router_kernel(x_ref, g_ref, sc_ref, sh_ref, wr_ref, o_ref, aff_ref):
    h = _normed(x_ref, g_ref, sc_ref, sh_ref)
    o_ref[...] = h
    logits = lax.dot_general(wr_ref[...], h.astype(BF16), (((1,), (1,)), ((), ())),
                             preferred_element_type=F32)
    e = jnp.exp(logits - jnp.max(logits, axis=0, keepdims=True))
    aff_ref[...] = e / jnp.sum(e, axis=0, keepdims=True)


def _seg_of(i, tm, n_lat, n_seg):
    return jnp.minimum((i * tm) // n_lat, n_seg)


def _norm_mod(x, g, mod, k_scale, k_shift, n_lat, n_seg, out_dtype, tm=256, x_ctx=None):
    d = x.shape[1]
    t = x.shape[0] + (0 if x_ctx is None else x_ctx.shape[0])
    seg = functools.partial(_seg_of, tm=tm, n_lat=n_lat, n_seg=n_seg)
    if x_ctx is None:
        kern, xs, xspecs = _norm_kernel, (x,), [pl.BlockSpec((tm, d), lambda i: (i, 0))]
    else:
        lat_tiles = x.shape[0] // tm
        lat, ctx = _split_rows(lat_tiles)
        kern = functools.partial(_norm2_kernel, lat_tiles=lat_tiles)
        xs, xspecs = (x, x_ctx), [pl.BlockSpec((tm, d), lat), pl.BlockSpec((tm, d), ctx)]
    return pl.pallas_call(
        kern,
        grid=(t // tm,),
        in_specs=xspecs + [
            pl.BlockSpec((1, d), lambda i: (0, 0)),
            pl.BlockSpec((None, 1, d), lambda i: (seg(i) * N_MOD + k_scale, 0, 0)),
            pl.BlockSpec((None, 1, d), lambda i: (seg(i) * N_MOD + k_shift, 0, 0))],
        out_specs=pl.BlockSpec((tm, d), lambda i: (i, 0)),
        out_shape=jax.ShapeDtypeStruct((t, d), out_dtype),
        compiler_params=_cparams(1),
        name="norm_mod",
    )(*xs, g, mod, mod)


def _norm_mod_router(x, g, mod, k_scale, k_shift, wr_t, n_lat, n_seg, tm=256):
    t, d = x.shape
    e = wr_t.shape[0]
    seg = functools.partial(_seg_of, tm=tm, n_lat=n_lat, n_seg=n_seg)
    return pl.pallas_call(
        _norm_router_kernel,
        grid=(t // tm,),
        in_specs=[pl.BlockSpec((tm, d), lambda i: (i, 0)),
                  pl.BlockSpec((1, d), lambda i: (0, 0)),
                  pl.BlockSpec((None, 1, d), lambda i: (seg(i) * N_MOD + k_scale, 0, 0)),
                  pl.BlockSpec((None, 1, d), lambda i: (seg(i) * N_MOD + k_shift, 0, 0)),
                  pl.BlockSpec((e, d), lambda i: (0, 0))],
        out_specs=[pl.BlockSpec((tm, d), lambda i: (i, 0)),
                   pl.BlockSpec((e, tm), lambda i: (0, i))],
        out_shape=[jax.ShapeDtypeStruct((t, d), F32),
                   jax.ShapeDtypeStruct((e, t), F32)],
        compiler_params=_cparams(1),
        name="norm_mod_router",
    )(x, g, mod, mod, wr_t)


def _final_norm_kernel(x_ref, g_ref, o_ref):
    x = x_ref[...]
    o_ref[...] = x * lax.rsqrt(jnp.mean(x * x, axis=-1, keepdims=True) + EPS) * g_ref[...]


def _final_norm(x, g, rows, tm=256):
    d = x.shape[1]
    return pl.pallas_call(
        _final_norm_kernel,
        grid=(rows // tm,),
        in_specs=[pl.BlockSpec((tm, d), lambda i: (i, 0)),
                  pl.BlockSpec((1, d), lambda i: (0, 0))],
        out_specs=pl.BlockSpec((tm, d), lambda i: (i, 0)),
        out_shape=jax.ShapeDtypeStruct((rows, d), F32),
        compiler_params=_cparams(1),
        name="final_norm",
    )(x, g)


def _cache_weight(w_ref, ws_ref, first):
    @pl.when(first)
    def _():
        ws_ref[...] = w_ref[...].astype(BF16)


def _mm_t_kernel(a_ref, bt_ref, o_ref, bs_ref):
    @pl.when(pl.program_id(1) == 0)
    def _():
        bs_ref[...] = bt_ref[0].T.astype(BF16)

    o_ref[...] = jnp.dot(a_ref[...], bs_ref[...], preferred_element_type=F32).astype(o_ref.dtype)


def _layer_spec(block, index_map, w, l):
    if w.ndim == len(block):
        return pl.BlockSpec(block, index_map)
    return pl.BlockSpec((None,) + block, lambda *g: (l,) + index_map(*g))


def _matmul_t(a, bt, l, col0, ncols, out_dtype=F32, tm=512, tn=512):
    t, k = a.shape
    tn = min(tn, ncols)
    assert col0 % 8 == 0 and t % tm == 0 and ncols % tn == 0
    return pl.pallas_call(
        _mm_t_kernel,
        grid=(ncols // tn, t // tm),
        in_specs=[pl.BlockSpec((tm, k), lambda j, i: (i, 0)),
                  pl.BlockSpec((pl.Element(1), pl.Element(tn), pl.Element(k)),
                               lambda j, i: (l, pl.multiple_of(col0 + j * tn, 8), 0))],
        out_specs=pl.BlockSpec((tm, tn), lambda j, i: (i, j)),
        out_shape=jax.ShapeDtypeStruct((t, ncols), out_dtype),
        scratch_shapes=[pltpu.VMEM((k, tn), BF16)],
        compiler_params=_cparams(2),
        name="matmul_t",
    )(a, bt)


def _merge_kernel(fl_ref, fc_ref, ol_ref, oc_ref, wa_ref, wb_ref, ga_ref, gb_ref, o_ref,
                  was_ref, wbs_ref, *, lat_tiles):
    i = pl.program_id(1)
    _cache_weight(wa_ref, was_ref, i == 0)
    _cache_weight(wb_ref, wbs_ref, i == 0)

    def compute(fa_ref, oa_ref):
        ya = jnp.dot(fa_ref[...], was_ref[...], preferred_element_type=F32)
        yb = jnp.dot(oa_ref[...], wbs_ref[...], preferred_element_type=F32)
        o_ref[...] = (_sigmoid(ga_ref[...]) * ya + _sigmoid(gb_ref[...]) * yb).astype(o_ref.dtype)

    @pl.when(i < lat_tiles)
    def _():
        compute(fl_ref, ol_ref)

    @pl.when(i >= lat_tiles)
    def _():
        compute(fc_ref, oc_ref)


def _merge(four_lat, four_ctx, ogla_lat, ogla_ctx, w_a, w_b, l, gates, tm=512, tn=512):
    ka, kb = four_lat.shape[1], ogla_lat.shape[1]
    lat_tiles = four_lat.shape[0] // tm
    t = gates.shape[0]
    d = w_a.shape[-1]
    nj = d // tn

    def lat(j, i):
        return (jnp.minimum(i, lat_tiles - 1), 0)

    def ctx(j, i):
        return (jnp.maximum(i - lat_tiles, 0), 0)

    return pl.pallas_call(
        functools.partial(_merge_kernel, lat_tiles=lat_tiles),
        grid=(nj, t // tm),
        in_specs=[pl.BlockSpec((tm, ka), lat),
                  pl.BlockSpec((tm, ka), ctx),
                  pl.BlockSpec((tm, kb), lat),
                  pl.BlockSpec((tm, kb), ctx),
                  _layer_spec((ka, tn), lambda j, i: (0, j), w_a, l),
                  _layer_spec((kb, tn), lambda j, i: (0, j), w_b, l),
                  pl.BlockSpec((tm, tn), lambda j, i: (i, j)),
                  pl.BlockSpec((tm, tn), lambda j, i: (i, nj + j))],
        out_specs=pl.BlockSpec((tm, tn), lambda j, i: (i, j)),
        out_shape=jax.ShapeDtypeStruct((t, d), BF16),
        scratch_shapes=[pltpu.VMEM((ka, tn), BF16), pltpu.VMEM((kb, tn), BF16)],
        compiler_params=_cparams(2),
        name="merge",
    )(four_lat, four_ctx, ogla_lat, ogla_ctx, w_a, w_b, gates, gates)


def _out_res_kernel(a_ref, b_ref, x_ref, gt_ref, o_ref, bs_ref):
    _cache_weight(b_ref, bs_ref, pl.program_id(1) == 0)
    y = jnp.dot(a_ref[...], bs_ref[...], preferred_element_type=F32)
    o_ref[...] = x_ref[...] + gt_ref[...] * y


def _out_res2_kernel(a_ref, b_ref, xl_ref, xc_ref, gt_ref, o_ref, bs_ref, *, lat_tiles):
    i = pl.program_id(1)
    _cache_weight(b_ref, bs_ref, i == 0)
    y = gt_ref[...] * jnp.dot(a_ref[...], bs_ref[...], preferred_element_type=F32)

    @pl.when(i < lat_tiles)
    def _():
        o_ref[...] = xl_ref[...] + y

    @pl.when(i >= lat_tiles)
    def _():
        o_ref[...] = xc_ref[...] + y


def _out_residual(m, w_out, l, x, mod, k_gate, n_lat, n_seg, tm=512, tn=512, x_ctx=None):
    t, k = m.shape
    d = w_out.shape[-1]
    seg = functools.partial(_seg_of, tm=tm, n_lat=n_lat, n_seg=n_seg)
    if x_ctx is None:
        kern, xs, xspecs = _out_res_kernel, (x,), [pl.BlockSpec((tm, tn), lambda j, i: (i, j))]
    else:
        lat_tiles = x.shape[0] // tm
        lat, ctx = _split_rows(lat_tiles)
        kern = functools.partial(_out_res2_kernel, lat_tiles=lat_tiles)
        xs = (x, x_ctx)
        xspecs = [pl.BlockSpec((tm, tn), lambda j, i: (lat(i)[0], j)),
                  pl.BlockSpec((tm, tn), lambda j, i: (ctx(i)[0], j))]
    return pl.pallas_call(
        kern,
        grid=(d // tn, t // tm),
        in_specs=[pl.BlockSpec((tm, k), lambda j, i: (i, 0)),
                  _layer_spec((k, tn), lambda j, i: (0, j), w_out, l)] + xspecs + [
            pl.BlockSpec((None, 1, tn), lambda j, i: (seg(i) * N_MOD + k_gate, 0, j))],
        out_specs=pl.BlockSpec((tm, tn), lambda j, i: (i, j)),
        out_shape=jax.ShapeDtypeStruct((t, d), F32),
        scratch_shapes=[pltpu.VMEM((k, tn), BF16)],
        compiler_params=_cparams(2),
        name="out_residual",
    )(m, w_out, *xs, mod)


def _chan_dft_kernel(u_ref, w_ref, p_ref, q_ref):
    c = p_ref.shape[1]
    y = jnp.dot(u_ref[...].astype(BF16), w_ref[...], preferred_element_type=F32)
    p_ref[...] = y[:, :c]
    q_ref[...] = y[:, c:]


def _chan_dft(z, wc, width, tm=512):
    t = z.shape[0]
    c = wc.shape[0]
    return pl.pallas_call(
        _chan_dft_kernel,
        grid=(t // tm, width // c),
        in_specs=[pl.BlockSpec((tm, c), lambda i, g: (i, g)),
                  pl.BlockSpec((c, 2 * c), lambda i, g: (0, 0))],
        out_specs=[pl.BlockSpec((tm, c), lambda i, g: (i, g)),
                   pl.BlockSpec((tm, c), lambda i, g: (i, g))],
        out_shape=[jax.ShapeDtypeStruct((t, width), F32)] * 2,
        compiler_params=_cparams(2),
        name="chan_dft",
    )(z, wc)


def _seq_dft_kernel(p_ref, q_ref, m1_ref, m2_ref, o_ref, bs_ref, os_ref, *, n1, scale, unroll):
    n2 = DFT_INNER

    def stage1(i, carry):
        xp = p_ref[pl.ds(i, n2, stride=n1), :]
        xq = q_ref[pl.ds(i, n2, stride=n1), :]
        xs = jnp.concatenate([xp, xq], axis=0).astype(BF16)
        row0 = pl.multiple_of(i * (2 * n2), 2 * n2)
        bs_ref[pl.ds(row0, 2 * n2), :] = jnp.dot(m1_ref[i], xs, preferred_element_type=F32)
        return carry

    lax.fori_loop(0, n1, stage1, 0, unroll=unroll)

    def stage2(k2, carry):
        are = bs_ref[pl.ds(k2, n1, stride=2 * n2), :]
        aim = bs_ref[pl.ds(n2 + k2, n1, stride=2 * n2), :]
        a = jnp.concatenate([are, aim], axis=0).astype(BF16)
        y = jnp.dot(m2_ref[...], a, preferred_element_type=F32) * scale
        os_ref[pl.ds(k2, n1, stride=n2), :] = y
        return carry

    lax.fori_loop(0, n2, stage2, 0, unroll=unroll)
    o_ref[...] = os_ref[...].astype(o_ref.dtype)


def _seq_dft(p, q, m1, m2, n_seq, n_seg, scale, ft=128, unroll=DFT_UNROLL):
    t, w = p.shape
    n1 = n_seq // DFT_INNER
    kern = functools.partial(_seq_dft_kernel, n1=n1, scale=scale, unroll=min(unroll, n1))
    return pl.pallas_call(
        kern,
        grid=(n_seg, w // ft),
        in_specs=[pl.BlockSpec((n_seq, ft), lambda b, j: (b, j)),
                  pl.BlockSpec((n_seq, ft), lambda b, j: (b, j)),
                  pl.BlockSpec(memory_space=pltpu.VMEM),
                  pl.BlockSpec(memory_space=pltpu.VMEM)],
        out_specs=pl.BlockSpec((n_seq, ft), lambda b, j: (b, j)),
        out_shape=jax.ShapeDtypeStruct((n_seg * n_seq, w), BF16),
        scratch_shapes=[pltpu.VMEM((n1 * 2 * DFT_INNER, ft), F32), pltpu.VMEM((n_seq, ft), F32)],
        compiler_params=_cparams(2),
        name="seq_dft",
    )(p, q, m1, m2)


def _ctx_dft_kernel(p_ref, q_ref, m_ref, o_ref, *, scale):
    a = jnp.concatenate([p_ref[...], q_ref[...]], axis=0).astype(BF16)
    o_ref[...] = (jnp.dot(m_ref[...], a, preferred_element_type=F32) * scale).astype(o_ref.dtype)


def _ctx_dft(p, q, mc, row_blk0, n_ctx, n_seg, scale, ft=512):
    w = p.shape[1]
    kern = functools.partial(_ctx_dft_kernel, scale=scale)
    spec = pl.BlockSpec((n_ctx, ft), lambda b, j: (row_blk0 + b, j))
    return pl.pallas_call(
        kern,
        grid=(n_seg, w // ft),
        in_specs=[spec, spec, pl.BlockSpec((n_ctx, 2 * n_ctx), lambda b, j: (0, 0))],
        out_specs=pl.BlockSpec((n_ctx, ft), lambda b, j: (b, j)),
        out_shape=jax.ShapeDtypeStruct((n_seg * n_ctx, w), BF16),
        compiler_params=_cparams(2),
        name="ctx_dft",
    )(p, q, mc)


def _dft_tables(n_seq, n_ctx, c):
    def cs(num, den):
        ang = (2.0 * math.pi / den) * (num % den).astype(F32)
        return jnp.cos(ang), jnp.sin(ang)

    ic = jnp.arange(c, dtype=jnp.int32)
    cc, sc = cs(ic[:, None] * ic[None, :], c)
    wc = jnp.concatenate([cc, -sc], axis=1).astype(BF16)

    n2 = DFT_INNER
    n1 = n_seq // n2
    i1 = jnp.arange(n1, dtype=jnp.int32)[:, None, None]
    k2 = jnp.arange(n2, dtype=jnp.int32)[None, :, None]
    j2 = jnp.arange(n2, dtype=jnp.int32)[None, None, :]
    c1, s1 = cs((i1 + n1 * j2) * k2, n_seq)
    m1 = jnp.concatenate([jnp.concatenate([c1, s1], axis=2),
                          jnp.concatenate([-s1, c1], axis=2)], axis=1).astype(BF16)
    k1 = jnp.arange(n1, dtype=jnp.int32)
    c2, s2 = cs(k1[:, None] * k1[None, :], n1)
    m2 = jnp.concatenate([c2, s2], axis=1).astype(BF16)

    il = jnp.arange(n_ctx, dtype=jnp.int32)
    cl, sl = cs(il[:, None] * il[None, :], n_ctx)
    mc = jnp.concatenate([cl, sl], axis=1).astype(BF16)
    return wc, m1, m2, mc


def _exact_tri_matmul(tri, v):
    h1 = v.astype(BF16)
    r1 = v - h1.astype(F32)
    h2 = r1.astype(BF16)
    h3 = (r1 - h2.astype(F32)).astype(BF16)
    return (jnp.dot(tri, h1, preferred_element_type=F32)
            + jnp.dot(tri, h2, preferred_element_type=F32)
            + jnp.dot(tri, h3, preferred_element_type=F32))


def _gla_kernel(qf_ref, kf_ref, vf_ref, lrf_ref, qb_ref, kb_ref, vb_ref, lrb_ref, wg_ref, bg_ref,
                s0_ref, of_ref, ob_ref, sout_ref, st_ref, *, rank, q_scale):
    t = pl.program_id(2)
    rows = qf_ref.shape[0]
    n_chunks = rows // CHUNK

    @pl.when(t == 0)
    def _():
        st_ref[...] = s0_ref[...]

    ii = lax.broadcasted_iota(jnp.int32, (rows, rows), 0)
    jj = lax.broadcasted_iota(jnp.int32, (rows, rows), 1)
    same_chunk = (ii // CHUNK) == (jj // CHUNK)
    dirs = ((qf_ref, kf_ref, vf_ref, lrf_ref, of_ref, same_chunk & (jj <= ii)),
            (qb_ref, kb_ref, vb_ref, lrb_ref, ob_ref, same_chunk & (jj >= ii)))

    pre = []
    for d, (q_ref, k_ref, v_ref, lr_ref, _, keep) in enumerate(dirs):
        lr = lr_ref[:, d * rank:(d + 1) * rank].astype(BF16)
        gl = jnp.dot(lr, wg_ref[d].astype(BF16), preferred_element_type=F32) + bg_ref[d]
        logg = (jnp.minimum(gl, 0.0) - jnp.log1p(jnp.exp(-jnp.abs(gl)))) * (1.0 / GATE_TAU)
        tri = jnp.where(keep, 1.0, 0.0).astype(BF16)
        cum = _exact_tri_matmul(tri, logg)
        tots = [cum[j * CHUNK:j * CHUNK + 1, :] if d else cum[(j + 1) * CHUNK - 1:(j + 1) * CHUNK, :]
                for j in range(n_chunks)]
        tot_rows = jnp.concatenate([jnp.broadcast_to(t_, (CHUNK, t_.shape[1])) for t_ in tots], axis=0)
        q = q_ref[...] * q_scale
        k = k_ref[...]
        vb = v_ref[...].astype(BF16)
        qd = (q * jnp.exp(cum)).astype(BF16)
        ki = (k * jnp.exp(-cum)).astype(BF16)
        kt = (k * jnp.exp(tot_rows - cum)).astype(BF16)
        att = lax.dot_general(qd, ki, (((1,), (1,)), ((), ())), preferred_element_type=F32)
        att = jnp.where(keep, att, 0.0).astype(BF16)
        o_intra = jnp.dot(att, vb, preferred_element_type=F32)
        pre.append((qd, kt, tots, o_intra))

    def chunk(d, j):
        v_ref, o_ref = dirs[d][2], dirs[d][4]
        qd, kt, tots, o_intra = pre[d]
        rs = slice(j * CHUNK, (j + 1) * CHUNK)
        st = st_ref[d]
        o_inter = lax.dot_general(qd[rs, :], st.astype(BF16), (((1,), (1,)), ((), ())),
                                  preferred_element_type=F32)
        o_ref[rs, :] = o_intra[rs, :] + o_inter
        upd = jnp.dot(v_ref[rs, :].T.astype(BF16), kt[rs, :], preferred_element_type=F32)
        st_ref[d] = st * jnp.exp(tots[j]) + upd

    for j in range(n_chunks):
        chunk(0, j)
        chunk(1, n_chunks - 1 - j)

    @pl.when(t == pl.num_programs(2) - 1)
    def _():
        sout_ref[...] = st_ref[...]


def _gla(z, lr, w_gate2, b_gate, s0, row0, seg_len, n_seg, q_col, k_col, v_col, dk, dv,
         rows=GLA_ROWS):
    heads = GLA_HEADS
    rank = w_gate2.shape[1]
    nt = seg_len // rows
    blk0 = row0 // rows

    def fwd(b, s):
        return b * nt + s

    def bwd(b, s):
        return b * nt + nt - 1 - s

    def zspec(width, col, blk):
        return pl.BlockSpec((rows, width), lambda b, h, s: (blk0 + blk(b, s), col // width + h))

    def lrspec(blk):
        return pl.BlockSpec((rows, lr.shape[1]), lambda b, h, s: (blk0 + blk(b, s), 0))

    state = pl.BlockSpec((None, None, N_DIR, dv, dk), lambda b, h, s: (b, h, 0, 0, 0))
    kern = functools.partial(_gla_kernel, rank=rank, q_scale=dk ** -0.5)
    return pl.pallas_call(
        kern,
        grid=(n_seg, heads, nt),
        in_specs=[zspec(dk, q_col, fwd), zspec(dk, k_col, fwd), zspec(dv, v_col, fwd), lrspec(fwd),
                  zspec(dk, q_col, bwd), zspec(dk, k_col, bwd), zspec(dv, v_col, bwd), lrspec(bwd),
                  pl.BlockSpec((N_DIR, rank, dk), lambda b, h, s: (0, 0, h)),
                  pl.BlockSpec((N_DIR, 1, dk), lambda b, h, s: (0, 0, h)),
                  state],
        out_specs=[pl.BlockSpec((rows, dv), lambda b, h, s: (fwd(b, s), h)),
                   pl.BlockSpec((rows, dv), lambda b, h, s: (bwd(b, s), h)),
                   state],
        out_shape=[jax.ShapeDtypeStruct((n_seg * seg_len, heads * dv), F32),
                   jax.ShapeDtypeStruct((n_seg * seg_len, heads * dv), F32),
                   jax.ShapeDtypeStruct((n_seg, heads, N_DIR, dv, dk), F32)],
        scratch_shapes=[pltpu.VMEM((N_DIR, dv, dk), F32)],
        compiler_params=_cparams(3),
        name="gla_scan",
    )(z, z, z, lr, z, z, z, lr, w_gate2, b_gate.reshape(N_DIR, 1, -1), s0)


def _gla_post_kernel(of_ref, ob_ref, r_ref, g_ref, o_ref, *, dv):
    o = of_ref[...] + ob_ref[...]
    r = r_ref[...]
    gate = g_ref[...] * (r * _sigmoid(r))
    for h in range(o.shape[1] // dv):
        cs = slice(h * dv, (h + 1) * dv)
        oh = o[:, cs]
        oh = oh * lax.rsqrt(jnp.mean(oh * oh, axis=-1, keepdims=True) + EPS)
        o_ref[:, cs] = (oh * gate[:, cs]).astype(o_ref.dtype)


def _gla_post(o_f, o_b, z, row0, g_head, r_col, dv, tm=256):
    t, gv = o_f.shape
    blk0 = row0 // tm
    kern = functools.partial(_gla_post_kernel, dv=dv)
    return pl.pallas_call(
        kern,
        grid=(t // tm,),
        in_specs=[pl.BlockSpec((tm, gv), lambda i: (i, 0)),
                  pl.BlockSpec((tm, gv), lambda i: (i, 0)),
                  pl.BlockSpec((tm, gv), lambda i: (blk0 + i, r_col // gv)),
                  pl.BlockSpec((1, gv), lambda i: (0, 0))],
        out_specs=pl.BlockSpec((tm, gv), lambda i: (i, 0)),
        out_shape=jax.ShapeDtypeStruct((t, gv), BF16),
        compiler_params=_cparams(1),
        name="gla_post",
    )(o_f, o_b, z, g_head)


_ROUTE_COLS = 16


def _route_kernel(aff_ref, idx_ref, gate_ref, c_ref, m_ref, row_ref, res_ref, win_ref, w0_ref,
                  *, cap):
    e = pl.program_id(1)
    n_exp, n = aff_ref.shape
    lanes = 128
    nblk = n // lanes

    @pl.when(e == 0)
    def _():
        a = aff_ref[...]
        bits = pltpu.bitcast(a, jnp.int32)
        thr = jnp.zeros((n_exp, 1), jnp.int32)
        def count(pred):
            return jnp.sum(jnp.where(pred, 1.0, 0.0), axis=1, keepdims=True)

        for b in range(30, -1, -1):
            cand = thr | (1 << b)
            thr = jnp.where(count(bits >= cand) >= cap, cand, thr)
        gt = bits > thr
        eq = bits == thr
        need = cap - count(gt)
        pos = lax.broadcasted_iota(jnp.int32, (n_exp, n), 1)
        bound = jnp.zeros((n_exp, 1), jnp.int32)
        for b in range(int(math.log2(n)), -1, -1):
            cand = bound + (1 << b)
            bound = jnp.where(count(eq & (pos < cand)) <= need, cand, bound)
        sel = gt | (eq & (pos < bound))
        m = jnp.where(sel, 1.0, 0.0)
        m_ref[...] = m
        ri = lax.broadcasted_iota(jnp.int32, (lanes, lanes), 0)
        ci = lax.broadcasted_iota(jnp.int32, (lanes, lanes), 1)
        tri = jnp.where(ri <= ci, 1.0, 0.0).astype(BF16)
        off = jnp.zeros((n_exp, 1), F32)
        for j in range(nblk):
            cj = jnp.dot(m[:, j * lanes:(j + 1) * lanes].astype(BF16), tri,
                         preferred_element_type=F32) + off
            c_ref[:, j * lanes:(j + 1) * lanes] = cj
            off = cj[:, lanes - 1:lanes]

    pick = lax.broadcasted_iota(jnp.int32, (n_exp, n), 0) == e
    for k, src in enumerate((c_ref, m_ref, aff_ref)):
        row_ref[k:k + 1, :] = jnp.sum(jnp.where(pick, src[...], 0.0), axis=0, keepdims=True)

    win = lanes + 8
    res_ref[...] = jnp.zeros_like(res_ref)
    lane_row = lax.broadcasted_iota(jnp.int32, (1, lanes), 1).astype(F32).astype(BF16)
    pad_rows = jnp.zeros((_ROUTE_COLS - 5, lanes), BF16)
    row_in_win = lax.broadcasted_iota(jnp.int32, (win, lanes), 0)

    def match(j, carry):
        col = pl.multiple_of(j * lanes, lanes)
        cj = row_ref[0:1, pl.ds(col, lanes)]
        mj = row_ref[1:2, pl.ds(col, lanes)]
        aj = row_ref[2:3, pl.ds(col, lanes)]
        before = (cj[0, 0] - mj[0, 0]).astype(jnp.int32)
        w0 = (before // 8) * 8
        slot1 = (row_in_win + (w0 + 1)).astype(F32)
        onehot = jnp.where((cj == slot1) & (mj > 0.5), 1.0, 0.0).astype(BF16)
        a1 = aj.astype(BF16)
        r1 = aj - a1.astype(F32)
        a2 = r1.astype(BF16)
        a3 = (r1 - a2.astype(F32)).astype(BF16)
        blk_row = jnp.full((1, lanes), j, F32).astype(BF16)
        vals = jnp.concatenate([lane_row, blk_row, a1, a2, a3, pad_rows], axis=0)
        y = lax.dot_general(onehot, vals, (((1,), (1,)), ((), ())), preferred_element_type=F32)
        win_ref[pl.ds(pl.multiple_of(j * win, 8), win), 0:_ROUTE_COLS] = y
        w0_ref[j] = w0
        return carry

    def place(j, carry):
        w0 = pl.multiple_of(w0_ref[j], 8)
        y = win_ref[pl.ds(pl.multiple_of(j * win, 8), win), 0:_ROUTE_COLS]
        res_ref[pl.ds(w0, win), 0:_ROUTE_COLS] = res_ref[pl.ds(w0, win), 0:_ROUTE_COLS] + y
        return carry

    lax.fori_loop(0, nblk, match, 0, unroll=min(4, nblk))
    lax.fori_loop(0, nblk, place, 0, unroll=2)
    res = res_ref[0:cap, :]
    lane = lax.broadcasted_iota(jnp.int32, res.shape, 1)
    tok = jnp.where(lane == 0, res, 0.0) + jnp.where(lane == 1, res * lanes, 0.0)
    idx_ref[...] = jnp.sum(tok, axis=1, keepdims=True).astype(jnp.int32)
    gate_ref[...] = jnp.sum(jnp.where((lane >= 2) & (lane <= 4), res, 0.0), axis=1, keepdims=True)


def _route(aff_t, col_blk0, n_set, n_sets, cap):
    n_exp = aff_t.shape[0]
    kern = functools.partial(_route_kernel, cap=cap)
    return pl.pallas_call(
        kern,
        grid=(n_sets, n_exp),
        in_specs=[pl.BlockSpec((n_exp, n_set), lambda s, e: (0, col_blk0 + s))],
        out_specs=[pl.BlockSpec((None, None, cap, 1), lambda s, e: (s, e, 0, 0)),
                   pl.BlockSpec((None, None, cap, 1), lambda s, e: (s, e, 0, 0))],
        out_shape=[jax.ShapeDtypeStruct((n_sets, n_exp, cap, 1), jnp.int32),
                   jax.ShapeDtypeStruct((n_sets, n_exp, cap, 1), F32)],
        scratch_shapes=[pltpu.VMEM((n_exp, n_set), F32), pltpu.VMEM((n_exp, n_set), F32),
                        pltpu.VMEM((8, n_set), F32), pltpu.VMEM((cap + 128 + 8, 128), F32),
                        pltpu.VMEM((n_set // 128 * (128 + 8), 128), F32),
                        pltpu.SMEM((n_set // 128,), jnp.int32)],
        compiler_params=_cparams(2),
        name="route",
    )(aff_t)


def _row_copy(src, src_row, dst, dst_row, sem):
    return pltpu.make_async_copy(src.at[pl.ds(src_row, 1), :], dst.at[pl.ds(dst_row, 1), :], sem)


def _gather_kernel(idx_ref, h_hbm, o_ref, buf, sem):
    n = buf.shape[1]
    k = pl.program_id(0)

    def fetch(r, kk):
        return _row_copy(h_hbm, idx_ref[kk * n + r], buf.at[kk % 2], r, sem.at[kk % 2])

    def each_row(fn):
        lax.fori_loop(0, n, lambda r, c: (fn(r), c)[1], 0, unroll=DMA_UNROLL)

    @pl.when(k == 0)
    def _():
        each_row(lambda r: fetch(r, k).start())

    @pl.when(k + 1 < pl.num_programs(0))
    def _():
        each_row(lambda r: fetch(r, k + 1).start())

    each_row(lambda r: fetch(r, k).wait())
    o_ref[...] = buf[k % 2].astype(o_ref.dtype)


def _gather_rows(idx, h, n=DMA_ROWS):
    r = idx.shape[0]
    n = min(n, r)
    d = h.shape[1]
    return pl.pallas_call(
        _gather_kernel,
        grid_spec=pltpu.PrefetchScalarGridSpec(
            num_scalar_prefetch=1,
            grid=(r // n,),
            in_specs=[pl.BlockSpec(memory_space=pl.ANY)],
            out_specs=pl.BlockSpec((n, d), lambda g, idx_ref: (g, 0)),
            scratch_shapes=[pltpu.VMEM((2, n, d), F32), pltpu.SemaphoreType.DMA((2,))]),
        out_shape=jax.ShapeDtypeStruct((r, d), BF16),
        compiler_params=_cparams(1),
        name="gather_rows",
    )(idx, h)


def _down_scatter_kernel(idx_ref, hid_ref, gate_ref, gt_ref, wd_hbm, x_in, x_hbm,
                         wf_ref, wb_ref, y_ref, xb_ref, wsem, rsem, osem, *, layer, tn, serial):
    del x_in
    e, i = pl.program_id(0), pl.program_id(1)
    n_e, n_i = pl.num_programs(0), pl.num_programs(1)
    n, d = y_ref.shape
    k = e * n_i + i
    slot = k % 2
    base = k * n

    def weight_copy(ee):
        return pltpu.make_async_copy(wd_hbm.at[layer, ee], wf_ref, wsem)

    def fetch(r, kk, sl):
        return _row_copy(x_hbm, idx_ref[kk * n + r], xb_ref.at[sl], r, rsem.at[sl])

    def put(r, kk, sl):
        return _row_copy(xb_ref.at[sl], r, x_hbm, idx_ref[kk * n + r], osem.at[sl])

    def each_row(fn):
        lax.fori_loop(0, n, lambda r, c: (fn(r), c)[1], 0, unroll=DMA_UNROLL)

    @pl.when(i == 0)
    def _():
        @pl.when(e == 0)
        def _():
            weight_copy(e).start()

        weight_copy(e).wait()
        wb_ref[...] = wf_ref[...].astype(BF16)

        @pl.when(e + 1 < n_e)
        def _():
            weight_copy(e + 1).start()

    if serial:
        @pl.when(k > 0)
        def _():
            each_row(lambda r: put(r, k - 1, 1 - slot).wait())

    each_row(lambda r: fetch(r, k, slot).start())
    hid = hid_ref[...]
    for p in range(d // tn):
        cs = slice(p * tn, (p + 1) * tn)
        y_ref[:, cs] = jnp.dot(hid, wb_ref[:, cs], preferred_element_type=F32) * gate_ref[...]
    each_row(lambda r: fetch(r, k, slot).wait())
    xb_ref[slot] = xb_ref[slot] + gt_ref[...] * y_ref[...]
    each_row(lambda r: put(r, k, slot).start())

    if not serial:
        @pl.when(k > 0)
        def _():
            each_row(lambda r: put(r, k - 1, 1 - slot).wait())

    @pl.when(k == n_e * n_i - 1)
    def _():
        each_row(lambda r: put(r, k, slot).wait())


def _expert_down_scatter(idx, hid, w_d, l, gates, x, mod, k_gate, rows_per_set, n_sets, mod_row,
                         tr=512, tn=1024):
    n_exp, r, ff = hid.shape
    d = w_d.shape[-1]
    tr = min(tr, rows_per_set)
    tn = min(tn, d)
    tiles_per_set = rows_per_set // tr
    assert rows_per_set % tr == 0 and r == n_sets * rows_per_set

    def gate_blk(e, i, idx_ref):
        row = (i // tiles_per_set) % n_sets if mod_row is None else mod_row
        return (row * N_MOD + k_gate, 0, 0)

    kern = functools.partial(_down_scatter_kernel, layer=l, tn=tn, serial=n_sets < 2)
    return pl.pallas_call(
        kern,
        grid_spec=pltpu.PrefetchScalarGridSpec(
            num_scalar_prefetch=1,
            grid=(n_exp, r // tr),
            in_specs=[pl.BlockSpec((None, tr, ff), lambda e, i, idx_ref: (e, i, 0)),
                      pl.BlockSpec((None, tr, 1), lambda e, i, idx_ref: (e, i, 0)),
                      pl.BlockSpec((None, 1, d), gate_blk),
                      pl.BlockSpec(memory_space=pl.ANY),
                      pl.BlockSpec(memory_space=pl.ANY)],
            out_specs=pl.BlockSpec(memory_space=pl.ANY),
            scratch_shapes=[pltpu.VMEM((ff, d), F32), pltpu.VMEM((ff, d), BF16),
                            pltpu.VMEM((tr, d), F32), pltpu.VMEM((2, tr, d), F32),
                            pltpu.SemaphoreType.DMA(()), pltpu.SemaphoreType.DMA((2,)),
                            pltpu.SemaphoreType.DMA((2,))]),
        out_shape=jax.ShapeDtypeStruct(x.shape, F32),
        input_output_aliases={5: 0},
        compiler_params=_cparams(2),
        name="expert_down_scatter",
    )(idx, hid, gates, mod, w_d, x)


def _expert_up_kernel(x_ref, wg_ref, wu_ref, o_ref, wgs_ref, wus_ref):
    first = pl.program_id(2) == 0
    _cache_weight(wg_ref, wgs_ref, first)
    _cache_weight(wu_ref, wus_ref, first)
    x = x_ref[...]
    g = jnp.dot(x, wgs_ref[...], preferred_element_type=F32)
    u = jnp.dot(x, wus_ref[...], preferred_element_type=F32)
    o_ref[...] = (g * _sigmoid(g) * u).astype(o_ref.dtype)


def _expert_up(xin, w_g, w_u, l, tr=1024, tf=256):
    n_exp, r, d = xin.shape
    ff = w_g.shape[-1]
    tr = min(tr, r)
    wspec = pl.BlockSpec((None, None, d, tf), lambda e, f, i: (l, e, 0, f))
    return pl.pallas_call(
        _expert_up_kernel,
        grid=(n_exp, ff // tf, r // tr),
        in_specs=[pl.BlockSpec((None, tr, d), lambda e, f, i: (e, i, 0)), wspec, wspec],
        out_specs=pl.BlockSpec((None, tr, tf), lambda e, f, i: (e, i, f)),
        out_shape=jax.ShapeDtypeStruct((n_exp, r, ff), BF16),
        scratch_shapes=[pltpu.VMEM((d, tf), BF16), pltpu.VMEM((d, tf), BF16)],
        compiler_params=_cparams(3),
        name="expert_up",
    )(xin, w_g, w_u)


def _moe(x, h2, aff_t, mod, k_gate, w_g, w_u, w_d, l, row0, n_set, n_sets, mod_row):
    n_exp = aff_t.shape[0]
    d = x.shape[1]
    cap = CAPACITY_FACTOR * n_set // n_exp
    idx, gates = _route(aff_t, row0 // n_set, n_set, n_sets, cap)
    offs = row0 + n_set * jnp.arange(n_sets, dtype=jnp.int32)
    rows = (idx.reshape(n_sets, n_exp, cap) + offs[:, None, None]).transpose(1, 0, 2).reshape(-1)
    gates = gates.reshape(n_sets, n_exp, cap).transpose(1, 0, 2).reshape(n_exp, n_sets * cap, 1)
    xin = _gather_rows(rows, h2).reshape(n_exp, n_sets * cap, d)
    hid = _expert_up(xin, w_g, w_u, l)
    return _expert_down_scatter(rows, hid, w_d, l, gates, x, mod, k_gate, cap, n_sets, mod_row)


def kernel(x, c, ctx, c_ctx, w_ada, b_ada, g_norm1, w_in, w_gate2, b_gate, g_head, w_branch_a,
           w_branch_b, w_out, g_norm2, w_router, w_exp_gate, w_exp_up, w_exp_down, g_final):
    bsz, n_lat, d = x.shape
    n_ctx = ctx.shape[1]
    depth = w_ada.shape[0]
    fw = w_branch_a.shape[1]
    gk = w_gate2.shape[3]
    gv = g_head.shape[1]
    rank = w_gate2.shape[2]
    dk, dv = gk // GLA_HEADS, gv // GLA_HEADS
    cgrp = fw // N_FOURIER_GROUPS
    lat_rows = bsz * n_lat
    q_col, k_col, v_col, r_col = fw, fw + gk, fw + 2 * gk, fw + 2 * gk + gv
    lr_col = r_col + gv
    gl_col = lr_col + N_DIR * rank
    lr_w = 128

    xs, xs_ctx = x.reshape(lat_rows, d), ctx.reshape(bsz * n_ctx, d)
    c_rows = jnp.concatenate([c, c_ctx[None, :], jnp.zeros((8 - bsz - 1, d), F32)], axis=0)
    mod_all = _adaln(c_rows, w_ada, b_ada)
    wc, m1, m2, mc = _dft_tables(n_lat, n_ctx, cgrp)
    zero_state = jnp.zeros((bsz, GLA_HEADS, N_DIR, dv, dk), F32)
    w_in_t = jnp.swapaxes(w_in, 1, 2)

    for l in range(depth):
        last = l == depth - 1
        mod = mod_all[l].reshape(8 * N_MOD, 1, d)
        h = _norm_mod(xs, g_norm1[l][None, :], mod, 1, 0, n_lat, bsz, BF16, x_ctx=xs_ctx)
        tm_in = 768 if h.shape[0] % 768 == 0 else 512
        z = _matmul_t(h, w_in_t, l, 0, lr_col, tm=tm_in)
        lr = _matmul_t(h, w_in_t, l, lr_col, lr_w)
        gl = _matmul_t(h, w_in_t, l, gl_col, 2 * d, tm=tm_in)
        p, q = _chan_dft(z, wc, fw)
        four_l = _seq_dft(p, q, m1, m2, n_lat, bsz, 1.0 / math.sqrt(n_lat * cgrp),
                          unroll=DFT_UNROLL * (1 + l % 2))
        four_c = _ctx_dft(p, q, mc, lat_rows // n_ctx, n_ctx, bsz, 1.0 / math.sqrt(n_ctx * cgrp))
        of_c, ob_c, s_c = _gla(z, lr, w_gate2[l], b_gate[l], zero_state, lat_rows, n_ctx, bsz,
                               q_col, k_col, v_col, dk, dv)
        of_l, ob_l, _ = _gla(z, lr, w_gate2[l], b_gate[l], s_c, 0, n_lat, bsz,
                             q_col, k_col, v_col, dk, dv, rows=GLA_ROWS // (1 + l % 2))
        ogla_l = _gla_post(of_l, ob_l, z, 0, g_head[l][None, :], r_col, dv)
        ogla_c = _gla_post(of_c, ob_c, z, lat_rows, g_head[l][None, :], r_col, dv)
        m = _merge(four_l, four_c, ogla_l, ogla_c, w_branch_a, w_branch_b, l, gl)
        xs = _out_residual(m, w_out, l, xs, mod, 2, n_lat, bsz, x_ctx=xs_ctx)
        xs_ctx = None
        wr_t = w_router[l].T.astype(BF16)
        h2, aff_t = _norm_mod_router(xs, g_norm2[l][None, :], mod, 4, 3, wr_t, n_lat, bsz)
        xs = _moe(xs, h2, aff_t, mod, 5, w_exp_gate, w_exp_up, w_exp_down, l,
                  0, n_lat, bsz, None)
        if not last:
            xs = _moe(xs, h2, aff_t, mod, 5, w_exp_gate, w_exp_up, w_exp_down, l,
                      lat_rows, n_ctx, bsz, bsz)
    return _final_norm(xs, g_final[None, :], lat_rows).reshape(bsz, n_lat, d)
```

```python
import functools
import math

import jax
import jax.numpy as jnp
from jax import lax
from jax.experimental import pallas as pl
from jax.experimental.pallas import tpu as pltpu

F32 = jnp.float32
BF16 = jnp.bfloat16

N_FOURIER_GROUPS = 4
GLA_HEADS = 4
GATE_TAU = 16.0
CHUNK = 64
N_DIR = 2
CAPACITY_FACTOR = 2
N_MOD = 6
EPS = 1e-6

LANES = 128
DFT_INNER = 128
DFT_UNROLL = 16
VMEM_LIMIT = 56 << 20
GLA_ROWS = 256
DMA_ROWS = 128
DMA_UNROLL = 8


def _cparams(n_axes):
    return pltpu.CompilerParams(dimension_semantics=("arbitrary",) * n_axes,
                                vmem_limit_bytes=VMEM_LIMIT)


def _sigmoid(v):
    return 1.0 / (1.0 + jnp.exp(-v))


def _ada_kernel(c_ref, w_ref, b_ref, o_ref):
    c = c_ref[...]
    a = (c * _sigmoid(c)).astype(BF16)
    o_ref[...] = jnp.dot(a, w_ref[...].astype(BF16), preferred_element_type=F32) + b_ref[...]


def _adaln(c_rows, w_ada, b_ada, tn=512):
    depth, d, n = w_ada.shape
    r = c_rows.shape[0]
    return pl.pallas_call(
        _ada_kernel,
        grid=(depth, n // tn),
        in_specs=[pl.BlockSpec((r, d), lambda l, j: (0, 0)),
                  pl.BlockSpec((None, d, tn), lambda l, j: (l, 0, j)),
                  pl.BlockSpec((None, 1, tn), lambda l, j: (l, 0, j))],
        out_specs=pl.BlockSpec((None, r, tn), lambda l, j: (l, 0, j)),
        out_shape=jax.ShapeDtypeStruct((depth, r, n), F32),
        compiler_params=_cparams(2),
        name="adaln",
    )(c_rows, w_ada, b_ada.reshape(depth, 1, n))


def _normed(x_ref, g_ref, sc_ref, sh_ref):
    x = x_ref[...]
    y = x * lax.rsqrt(jnp.mean(x * x, axis=-1, keepdims=True) + EPS) * g_ref[...]
    return y * (1.0 + sc_ref[...]) + sh_ref[...]


def _norm_kernel(x_ref, g_ref, sc_ref, sh_ref, o_ref):
    o_ref[...] = _normed(x_ref, g_ref, sc_ref, sh_ref).astype(o_ref.dtype)


def _norm2_kernel(xl_ref, xc_ref, g_ref, sc_ref, sh_ref, o_ref, *, lat_tiles):
    @pl.when(pl.program_id(0) < lat_tiles)
    def _():
        o_ref[...] = _normed(xl_ref, g_ref, sc_ref, sh_ref).astype(o_ref.dtype)

    @pl.when(pl.program_id(0) >= lat_tiles)
    def _():
        o_ref[...] = _normed(xc_ref, g_ref, sc_ref, sh_ref).astype(o_ref.dtype)


def _split_rows(lat_tiles):
    def lat(i):
        return (jnp.minimum(i, lat_tiles - 1), 0)

    def ctx(i):
        return (jnp.maximum(i - lat_tiles, 0), 0)

    return lat, ctx


def _norm_router_kernel(x_ref, g_ref, sc_ref, sh_ref, wr_ref, o_ref, aff_ref):
    h = _normed(x_ref, g_ref, sc_ref, sh_ref)
    o_ref[...] = h
    logits = lax.dot_general(wr_ref[...], h.astype(BF16), (((1,), (1,)), ((), ())),
                             preferred_element_type=F32)
    e = jnp.exp(logits - jnp.max(logits, axis=0, keepdims=True))
    aff_ref[...] = e / jnp.sum(e, axis=0, keepdims=True)


def _seg_of(i, tm, n_lat, n_seg):
    return jnp.minimum((i * tm) // n_lat, n_seg)


def _norm_mod(x, g, mod, k_scale, k_shift, n_lat, n_seg, out_dtype, tm=256, x_ctx=None):
    d = x.shape[1]
    t = x.shape[0] + (0 if x_ctx is None else x_ctx.shape[0])
    seg = functools.partial(_seg_of, tm=tm, n_lat=n_lat, n_seg=n_seg)
    if x_ctx is None:
        kern, xs, xspecs = _norm_kernel, (x,), [pl.BlockSpec((tm, d), lambda i: (i, 0))]
    else:
        lat_tiles = x.shape[0] // tm
        lat, ctx = _split_rows(lat_tiles)
        kern = functools.partial(_norm2_kernel, lat_tiles=lat_tiles)
        xs, xspecs = (x, x_ctx), [pl.BlockSpec((tm, d), lat), pl.BlockSpec((tm, d), ctx)]
    return pl.pallas_call(
        kern,
        grid=(t // tm,),
        in_specs=xspecs + [
            pl.BlockSpec((1, d), lambda i: (0, 0)),
            pl.BlockSpec((None, 1, d), lambda i: (seg(i) * N_MOD + k_scale, 0, 0)),
            pl.BlockSpec((None, 1, d), lambda i: (seg(i) * N_MOD + k_shift, 0, 0))],
        out_specs=pl.BlockSpec((tm, d), lambda i: (i, 0)),
        out_shape=jax.ShapeDtypeStruct((t, d), out_dtype),
        compiler_params=_cparams(1),
        name="norm_mod",
    )(*xs, g, mod, mod)


def _norm_mod_router(x, g, mod, k_scale, k_shift, wr_t, n_lat, n_seg, tm=256):
    t, d = x.shape
    e = wr_t.shape[0]
    seg = functools.partial(_seg_of, tm=tm, n_lat=n_lat, n_seg=n_seg)
    return pl.pallas_call(
        _norm_router_kernel,
        grid=(t // tm,),
        in_specs=[pl.BlockSpec((tm, d), lambda i: (i, 0)),
                  pl.BlockSpec((1, d), lambda i: (0, 0)),
                  pl.BlockSpec((None, 1, d), lambda i: (seg(i) * N_MOD + k_scale, 0, 0)),
                  pl.BlockSpec((None, 1, d), lambda i: (seg(i) * N_MOD + k_shift, 0, 0)),
                  pl.BlockSpec((e, d), lambda i: (0, 0))],
        out_specs=[pl.BlockSpec((tm, d), lambda i: (i, 0)),
                   pl.BlockSpec((e, tm), lambda i: (0, i))],
        out_shape=[jax.ShapeDtypeStruct((t, d), F32),
                   jax.ShapeDtypeStruct((e, t), F32)],
        compiler_params=_cparams(1),
        name="norm_mod_router",
    )(x, g, mod, mod, wr_t)


def _final_norm_kernel(x_ref, g_ref, o_ref):
    x = x_ref[...]
    o_ref[...] = x * lax.rsqrt(jnp.mean(x * x, axis=-1, keepdims=True) + EPS) * g_ref[...]


def _final_norm(x, g, rows, tm=256):
    d = x.shape[1]
    return pl.pallas_call(
        _final_norm_kernel,
        grid=(rows // tm,),
        in_specs=[pl.BlockSpec((tm, d), lambda i: (i, 0)),
                  pl.BlockSpec((1, d), lambda i: (0, 0))],
        out_specs=pl.BlockSpec((tm, d), lambda i: (i, 0)),
        out_shape=jax.ShapeDtypeStruct((rows, d), F32),
        compiler_params=_cparams(1),
        name="final_norm",
    )(x, g)


def _cache_weight(w_ref, ws_ref, first):
    @pl.when(first)
    def _():
        ws_ref[...] = w_ref[...].astype(BF16)


def _mm_t_kernel(a_ref, bt_ref, o_ref, bs_ref):
    @pl.when(pl.program_id(1) == 0)
    def _():
        bs_ref[...] = bt_ref[0].T.astype(BF16)

    o_ref[...] = jnp.dot(a_ref[...], bs_ref[...], preferred_element_type=F32).astype(o_ref.dtype)


def _layer_spec(block, index_map, w, l):
    if w.ndim == len(block):
        return pl.BlockSpec(block, index_map)
    return pl.BlockSpec((None,) + block, lambda *g: (l,) + index_map(*g))


def _matmul_t(a, bt, l, col0, ncols, out_dtype=F32, tm=512, tn=512):
    t, k = a.shape
    tn = min(tn, ncols)
    assert col0 % 8 == 0 and t % tm == 0 and ncols % tn == 0
    return pl.pallas_call(
        _mm_t_kernel,
        grid=(ncols // tn, t // tm),
        in_specs=[pl.BlockSpec((tm, k), lambda j, i: (i, 0)),
                  pl.BlockSpec((pl.Element(1), pl.Element(tn), pl.Element(k)),
                               lambda j, i: (l, pl.multiple_of(col0 + j * tn, 8), 0))],
        out_specs=pl.BlockSpec((tm, tn), lambda j, i: (i, j)),
        out_shape=jax.ShapeDtypeStruct((t, ncols), out_dtype),
        scratch_shapes=[pltpu.VMEM((k, tn), BF16)],
        compiler_params=_cparams(2),
        name="matmul_t",
    )(a, bt)


def _merge_kernel(fl_ref, fc_ref, ol_ref, oc_ref, wa_ref, wb_ref, ga_ref, gb_ref, o_ref,
                  was_ref, wbs_ref, *, lat_tiles):
    i = pl.program_id(1)
    _cache_weight(wa_ref, was_ref, i == 0)
    _cache_weight(wb_ref, wbs_ref, i == 0)

    def compute(fa_ref, oa_ref):
        ya = jnp.dot(fa_ref[...], was_ref[...], preferred_element_type=F32)
        yb = jnp.dot(oa_ref[...], wbs_ref[...], preferred_element_type=F32)
        o_ref[...] = (_sigmoid(ga_ref[...]) * ya + _sigmoid(gb_ref[...]) * yb).astype(o_ref.dtype)

    @pl.when(i < lat_tiles)
    def _():
        compute(fl_ref, ol_ref)

    @pl.when(i >= lat_tiles)
    def _():
        compute(fc_ref, oc_ref)


def _merge(four_lat, four_ctx, ogla_lat, ogla_ctx, w_a, w_b, l, gates, tm=512, tn=512):
    ka, kb = four_lat.shape[1], ogla_lat.shape[1]
    lat_tiles = four_lat.shape[0] // tm
    t = gates.shape[0]
    d = w_a.shape[-1]
    nj = d // tn

    def lat(j, i):
        return (jnp.minimum(i, lat_tiles - 1), 0)

    def ctx(j, i):
        return (jnp.maximum(i - lat_tiles, 0), 0)

    return pl.pallas_call(
        functools.partial(_merge_kernel, lat_tiles=lat_tiles),
        grid=(nj, t // tm),
        in_specs=[pl.BlockSpec((tm, ka), lat),
                  pl.BlockSpec((tm, ka), ctx),
                  pl.BlockSpec((tm, kb), lat),
                  pl.BlockSpec((tm, kb), ctx),
                  _layer_spec((ka, tn), lambda j, i: (0, j), w_a, l),
                  _layer_spec((kb, tn), lambda j, i: (0, j), w_b, l),
                  pl.BlockSpec((tm, tn), lambda j, i: (i, j)),
                  pl.BlockSpec((tm, tn), lambda j, i: (i, nj + j))],
        out_specs=pl.BlockSpec((tm, tn), lambda j, i: (i, j)),
        out_shape=jax.ShapeDtypeStruct((t, d), BF16),
        scratch_shapes=[pltpu.VMEM((ka, tn), BF16), pltpu.VMEM((kb, tn), BF16)],
        compiler_params=_cparams(2),
        name="merge",
    )(four_lat, four_ctx, ogla_lat, ogla_ctx, w_a, w_b, gates, gates)


def _out_res_kernel(a_ref, b_ref, x_ref, gt_ref, o_ref, bs_ref):
    _cache_weight(b_ref, bs_ref, pl.program_id(1) == 0)
    y = jnp.dot(a_ref[...], bs_ref[...], preferred_element_type=F32)
    o_ref[...] = x_ref[...] + gt_ref[...] * y


def _out_res2_kernel(a_ref, b_ref, xl_ref, xc_ref, gt_ref, o_ref, bs_ref, *, lat_tiles):
    i = pl.program_id(1)
    _cache_weight(b_ref, bs_ref, i == 0)
    y = gt_ref[...] * jnp.dot(a_ref[...], bs_ref[...], preferred_element_type=F32)

    @pl.when(i < lat_tiles)
    def _():
        o_ref[...] = xl_ref[...] + y

    @pl.when(i >= lat_tiles)
    def _():
        o_ref[...] = xc_ref[...] + y


def _out_residual(m, w_out, l, x, mod, k_gate, n_lat, n_seg, tm=512, tn=512, x_ctx=None):
    t, k = m.shape
    d = w_out.shape[-1]
    seg = functools.partial(_seg_of, tm=tm, n_lat=n_lat, n_seg=n_seg)
    if x_ctx is None:
        kern, xs, xspecs = _out_res_kernel, (x,), [pl.BlockSpec((tm, tn), lambda j, i: (i, j))]
    else:
        lat_tiles = x.shape[0] // tm
        lat, ctx = _split_rows(lat_tiles)
        kern = functools.partial(_out_res2_kernel, lat_tiles=lat_tiles)
        xs = (x, x_ctx)
        xspecs = [pl.BlockSpec((tm, tn), lambda j, i: (lat(i)[0], j)),
                  pl.BlockSpec((tm, tn), lambda j, i: (ctx(i)[0], j))]
    return pl.pallas_call(
        kern,
        grid=(d // tn, t // tm),
        in_specs=[pl.BlockSpec((tm, k), lambda j, i: (i, 0)),
                  _layer_spec((k, tn), lambda j, i: (0, j), w_out, l)] + xspecs + [
            pl.BlockSpec((None, 1, tn), lambda j, i: (seg(i) * N_MOD + k_gate, 0, j))],
        out_specs=pl.BlockSpec((tm, tn), lambda j, i: (i, j)),
        out_shape=jax.ShapeDtypeStruct((t, d), F32),
        scratch_shapes=[pltpu.VMEM((k, tn), BF16)],
        compiler_params=_cparams(2),
        name="out_residual",
    )(m, w_out, *xs, mod)


def _chan_dft_kernel(u_ref, w_ref, p_ref, q_ref):
    c = p_ref.shape[1]
    y = jnp.dot(u_ref[...].astype(BF16), w_ref[...], preferred_element_type=F32)
    p_ref[...] = y[:, :c]
    q_ref[...] = y[:, c:]


def _chan_dft(z, wc, width, tm=512):
    t = z.shape[0]
    c = wc.shape[0]
    return pl.pallas_call(
        _chan_dft_kernel,
        grid=(t // tm, width // c),
        in_specs=[pl.BlockSpec((tm, c), lambda i, g: (i, g)),
                  pl.BlockSpec((c, 2 * c), lambda i, g: (0, 0))],
        out_specs=[pl.BlockSpec((tm, c), lambda i, g: (i, g)),
                   pl.BlockSpec((tm, c), lambda i, g: (i, g))],
        out_shape=[jax.ShapeDtypeStruct((t, width), F32)] * 2,
        compiler_params=_cparams(2),
        name="chan_dft",
    )(z, wc)


def _seq_dft_kernel(p_ref, q_ref, m1_ref, m2_ref, o_ref, bs_ref, os_ref, *, n1, scale, unroll):
    n2 = DFT_INNER

    def stage1(i, carry):
        xp = p_ref[pl.ds(i, n2, stride=n1), :]
        xq = q_ref[pl.ds(i, n2, stride=n1), :]
        xs = jnp.concatenate([xp, xq], axis=0).astype(BF16)
        row0 = pl.multiple_of(i * (2 * n2), 2 * n2)
        bs_ref[pl.ds(row0, 2 * n2), :] = jnp.dot(m1_ref[i], xs, preferred_element_type=F32)
        return carry

    lax.fori_loop(0, n1, stage1, 0, unroll=unroll)

    def stage2(k2, carry):
        are = bs_ref[pl.ds(k2, n1, stride=2 * n2), :]
        aim = bs_ref[pl.ds(n2 + k2, n1, stride=2 * n2), :]
        a = jnp.concatenate([are, aim], axis=0).astype(BF16)
        y = jnp.dot(m2_ref[...], a, preferred_element_type=F32) * scale
        os_ref[pl.ds(k2, n1, stride=n2), :] = y
        return carry

    lax.fori_loop(0, n2, stage2, 0, unroll=unroll)
    o_ref[...] = os_ref[...].astype(o_ref.dtype)


def _seq_dft(p, q, m1, m2, n_seq, n_seg, scale, ft=128, unroll=DFT_UNROLL):
    t, w = p.shape
    n1 = n_seq // DFT_INNER
    kern = functools.partial(_seq_dft_kernel, n1=n1, scale=scale, unroll=min(unroll, n1))
    return pl.pallas_call(
        kern,
        grid=(n_seg, w // ft),
        in_specs=[pl.BlockSpec((n_seq, ft), lambda b, j: (b, j)),
                  pl.BlockSpec((n_seq, ft), lambda b, j: (b, j)),
                  pl.BlockSpec(memory_space=pltpu.VMEM),
                  pl.BlockSpec(memory_space=pltpu.VMEM)],
        out_specs=pl.BlockSpec((n_seq, ft), lambda b, j: (b, j)),
        out_shape=jax.ShapeDtypeStruct((n_seg * n_seq, w), BF16),
        scratch_shapes=[pltpu.VMEM((n1 * 2 * DFT_INNER, ft), F32), pltpu.VMEM((n_seq, ft), F32)],
        compiler_params=_cparams(2),
        name="seq_dft",
    )(p, q, m1, m2)


def _ctx_dft_kernel(p_ref, q_ref, m_ref, o_ref, *, scale):
    a = jnp.concatenate([p_ref[...], q_ref[...]], axis=0).astype(BF16)
    o_ref[...] = (jnp.dot(m_ref[...], a, preferred_element_type=F32) * scale).astype(o_ref.dtype)


def _ctx_dft(p, q, mc, row_blk0, n_ctx, n_seg, scale, ft=512):
    w = p.shape[1]
    kern = functools.partial(_ctx_dft_kernel, scale=scale)
    spec = pl.BlockSpec((n_ctx, ft), lambda b, j: (row_blk0 + b, j))
    return pl.pallas_call(
        kern,
        grid=(n_seg, w // ft),
        in_specs=[spec, spec, pl.BlockSpec((n_ctx, 2 * n_ctx), lambda b, j: (0, 0))],
        out_specs=pl.BlockSpec((n_ctx, ft), lambda b, j: (b, j)),
        out_shape=jax.ShapeDtypeStruct((n_seg * n_ctx, w), BF16),
        compiler_params=_cparams(2),
        name="ctx_dft",
    )(p, q, mc)


def _dft_tables(n_seq, n_ctx, c):
    def cs(num, den):
        ang = (2.0 * math.pi / den) * (num % den).astype(F32)
        return jnp.cos(ang), jnp.sin(ang)

    ic = jnp.arange(c, dtype=jnp.int32)
    cc, sc = cs(ic[:, None] * ic[None, :], c)
    wc = jnp.concatenate([cc, -sc], axis=1).astype(BF16)

    n2 = DFT_INNER
    n1 = n_seq // n2
    i1 = jnp.arange(n1, dtype=jnp.int32)[:, None, None]
    k2 = jnp.arange(n2, dtype=jnp.int32)[None, :, None]
    j2 = jnp.arange(n2, dtype=jnp.int32)[None, None, :]
    c1, s1 = cs((i1 + n1 * j2) * k2, n_seq)
    m1 = jnp.concatenate([jnp.concatenate([c1, s1], axis=2),
                          jnp.concatenate([-s1, c1], axis=2)], axis=1).astype(BF16)
    k1 = jnp.arange(n1, dtype=jnp.int32)
    c2, s2 = cs(k1[:, None] * k1[None, :], n1)
    m2 = jnp.concatenate([c2, s2], axis=1).astype(BF16)

    il = jnp.arange(n_ctx, dtype=jnp.int32)
    cl, sl = cs(il[:, None] * il[None, :], n_ctx)
    mc = jnp.concatenate([cl, sl], axis=1).astype(BF16)
    return wc, m1, m2, mc


def _exact_tri_matmul(tri, v):
    h1 = v.astype(BF16)
    r1 = v - h1.astype(F32)
    h2 = r1.astype(BF16)
    h3 = (r1 - h2.astype(F32)).astype(BF16)
    return (jnp.dot(tri, h1, preferred_element_type=F32)
            + jnp.dot(tri, h2, preferred_element_type=F32)
            + jnp.dot(tri, h3, preferred_element_type=F32))


def _gla_kernel(qf_ref, kf_ref, vf_ref, lrf_ref, qb_ref, kb_ref, vb_ref, lrb_ref, wg_ref, bg_ref,
                s0_ref, of_ref, ob_ref, sout_ref, st_ref, *, rank, q_scale):
    t = pl.program_id(2)
    rows = qf_ref.shape[0]
    n_chunks = rows // CHUNK

    @pl.when(t == 0)
    def _():
        st_ref[...] = s0_ref[...]

    ii = lax.broadcasted_iota(jnp.int32, (rows, rows), 0)
    jj = lax.broadcasted_iota(jnp.int32, (rows, rows), 1)
    same_chunk = (ii // CHUNK) == (jj // CHUNK)
    dirs = ((qf_ref, kf_ref, vf_ref, lrf_ref, of_ref, same_chunk & (jj <= ii)),
            (qb_ref, kb_ref, vb_ref, lrb_ref, ob_ref, same_chunk & (jj >= ii)))

    pre = []
    for d, (q_ref, k_ref, v_ref, lr_ref, _, keep) in enumerate(dirs):
        lr = lr_ref[:, d * rank:(d + 1) * rank].astype(BF16)
        gl = jnp.dot(lr, wg_ref[d].astype(BF16), preferred_element_type=F32) + bg_ref[d]
        logg = (jnp.minimum(gl, 0.0) - jnp.log1p(jnp.exp(-jnp.abs(gl)))) * (1.0 / GATE_TAU)
        tri = jnp.where(keep, 1.0, 0.0).astype(BF16)
        cum = _exact_tri_matmul(tri, logg)
        tots = [cum[j * CHUNK:j * CHUNK + 1, :] if d else cum[(j + 1) * CHUNK - 1:(j + 1) * CHUNK, :]
                for j in range(n_chunks)]
        tot_rows = jnp.concatenate([jnp.broadcast_to(t_, (CHUNK, t_.shape[1])) for t_ in tots], axis=0)
        q = q_ref[...] * q_scale
        k = k_ref[...]
        vb = v_ref[...].astype(BF16)
        qd = (q * jnp.exp(cum)).astype(BF16)
        ki = (k * jnp.exp(-cum)).astype(BF16)
        kt = (k * jnp.exp(tot_rows - cum)).astype(BF16)
        att = lax.dot_general(qd, ki, (((1,), (1,)), ((), ())), preferred_element_type=F32)
        att = jnp.where(keep, att, 0.0).astype(BF16)
        o_intra = jnp.dot(att, vb, preferred_element_type=F32)
        pre.append((qd, kt, tots, o_intra))

    def chunk(d, j):
        v_ref, o_ref = dirs[d][2], dirs[d][4]
        qd, kt, tots, o_intra = pre[d]
        rs = slice(j * CHUNK, (j + 1) * CHUNK)
        st = st_ref[d]
        o_inter = lax.dot_general(qd[rs, :], st.astype(BF16), (((1,), (1,)), ((), ())),
                                  preferred_element_type=F32)
        o_ref[rs, :] = o_intra[rs, :] + o_inter
        upd = jnp.dot(v_ref[rs, :].T.astype(BF16), kt[rs, :], preferred_element_type=F32)
        st_ref[d] = st * jnp.exp(tots[j]) + upd

    for j in range(n_chunks):
        chunk(0, j)
        chunk(1, n_chunks - 1 - j)

    @pl.when(t == pl.num_programs(2) - 1)
    def _():
        sout_ref[...] = st_ref[...]


def _gla(z, lr, w_gate2, b_gate, s0, row0, seg_len, n_seg, q_col, k_col, v_col, dk, dv,
         rows=GLA_ROWS):
    heads = GLA_HEADS
    rank = w_gate2.shape[1]
    nt = seg_len // rows
    blk0 = row0 // rows

    def fwd(b, s):
        return b * nt + s

    def bwd(b, s):
        return b * nt + nt - 1 - s

    def zspec(width, col, blk):
        return pl.BlockSpec((rows, width), lambda b, h, s: (blk0 + blk(b, s), col // width + h))

    def lrspec(blk):
        return pl.BlockSpec((rows, lr.shape[1]), lambda b, h, s: (blk0 + blk(b, s), 0))

    state = pl.BlockSpec((None, None, N_DIR, dv, dk), lambda b, h, s: (b, h, 0, 0, 0))
    kern = functools.partial(_gla_kernel, rank=rank, q_scale=dk ** -0.5)
    return pl.pallas_call(
        kern,
        grid=(n_seg, heads, nt),
        in_specs=[zspec(dk, q_col, fwd), zspec(dk, k_col, fwd), zspec(dv, v_col, fwd), lrspec(fwd),
                  zspec(dk, q_col, bwd), zspec(dk, k_col, bwd), zspec(dv, v_col, bwd), lrspec(bwd),
                  pl.BlockSpec((N_DIR, rank, dk), lambda b, h, s: (0, 0, h)),
                  pl.BlockSpec((N_DIR, 1, dk), lambda b, h, s: (0, 0, h)),
                  state],
        out_specs=[pl.BlockSpec((rows, dv), lambda b, h, s: (fwd(b, s), h)),
                   pl.BlockSpec((rows, dv), lambda b, h, s: (bwd(b, s), h)),
                   state],
        out_shape=[jax.ShapeDtypeStruct((n_seg * seg_len, heads * dv), F32),
                   jax.ShapeDtypeStruct((n_seg * seg_len, heads * dv), F32),
                   jax.ShapeDtypeStruct((n_seg, heads, N_DIR, dv, dk), F32)],
        scratch_shapes=[pltpu.VMEM((N_DIR, dv, dk), F32)],
        compiler_params=_cparams(3),
        name="gla_scan",
    )(z, z, z, lr, z, z, z, lr, w_gate2, b_gate.reshape(N_DIR, 1, -1), s0)


def _gla_post_kernel(of_ref, ob_ref, r_ref, g_ref, o_ref, *, dv):
    o = of_ref[...] + ob_ref[...]
    r = r_ref[...]
    gate = g_ref[...] * (r * _sigmoid(r))
    for h in range(o.shape[1] // dv):
        cs = slice(h * dv, (h + 1) * dv)
        oh = o[:, cs]
        oh = oh * lax.rsqrt(jnp.mean(oh * oh, axis=-1, keepdims=True) + EPS)
        o_ref[:, cs] = (oh * gate[:, cs]).astype(o_ref.dtype)


def _gla_post(o_f, o_b, z, row0, g_head, r_col, dv, tm=256):
    t, gv = o_f.shape
    blk0 = row0 // tm
    kern = functools.partial(_gla_post_kernel, dv=dv)
    return pl.pallas_call(
        kern,
        grid=(t // tm,),
        in_specs=[pl.BlockSpec((tm, gv), lambda i: (i, 0)),
                  pl.BlockSpec((tm, gv), lambda i: (i, 0)),
                  pl.BlockSpec((tm, gv), lambda i: (blk0 + i, r_col // gv)),
                  pl.BlockSpec((1, gv), lambda i: (0, 0))],
        out_specs=pl.BlockSpec((tm, gv), lambda i: (i, 0)),
        out_shape=jax.ShapeDtypeStruct((t, gv), BF16),
        compiler_params=_cparams(1),
        name="gla_post",
    )(o_f, o_b, z, g_head)


_ROUTE_COLS = 16


def _route_kernel(aff_ref, idx_ref, gate_ref, c_ref, m_ref, row_ref, res_ref, win_ref, w0_ref,
                  *, cap):
    e = pl.program_id(1)
    n_exp, n = aff_ref.shape
    lanes = 128
    nblk = n // lanes

    @pl.when(e == 0)
    def _():
        a = aff_ref[...]
        bits = pltpu.bitcast(a, jnp.int32)
        thr = jnp.zeros((n_exp, 1), jnp.int32)
        def count(pred):
            return jnp.sum(jnp.where(pred, 1.0, 0.0), axis=1, keepdims=True)

        for b in range(30, -1, -1):
            cand = thr | (1 << b)
            thr = jnp.where(count(bits >= cand) >= cap, cand, thr)
        gt = bits > thr
        eq = bits == thr
        need = cap - count(gt)
        pos = lax.broadcasted_iota(jnp.int32, (n_exp, n), 1)
        bound = jnp.zeros((n_exp, 1), jnp.int32)
        for b in range(int(math.log2(n)), -1, -1):
            cand = bound + (1 << b)
            bound = jnp.where(count(eq & (pos < cand)) <= need, cand, bound)
        sel = gt | (eq & (pos < bound))
        m = jnp.where(sel, 1.0, 0.0)
        m_ref[...] = m
        ri = lax.broadcasted_iota(jnp.int32, (lanes, lanes), 0)
        ci = lax.broadcasted_iota(jnp.int32, (lanes, lanes), 1)
        tri = jnp.where(ri <= ci, 1.0, 0.0).astype(BF16)
        off = jnp.zeros((n_exp, 1), F32)
        for j in range(nblk):
            cj = jnp.dot(m[:, j * lanes:(j + 1) * lanes].astype(BF16), tri,
                         preferred_element_type=F32) + off
            c_ref[:, j * lanes:(j + 1) * lanes] = cj
            off = cj[:, lanes - 1:lanes]

    pick = lax.broadcasted_iota(jnp.int32, (n_exp, n), 0) == e
    for k, src in enumerate((c_ref, m_ref, aff_ref)):
        row_ref[k:k + 1, :] = jnp.sum(jnp.where(pick, src[...], 0.0), axis=0, keepdims=True)

    win = lanes + 8
    res_ref[...] = jnp.zeros_like(res_ref)
    lane_row = lax.broadcasted_iota(jnp.int32, (1, lanes), 1).astype(F32).astype(BF16)
    pad_rows = jnp.zeros((_ROUTE_COLS - 5, lanes), BF16)
    row_in_win = lax.broadcasted_iota(jnp.int32, (win, lanes), 0)

    def match(j, carry):
        col = pl.multiple_of(j * lanes, lanes)
        cj = row_ref[0:1, pl.ds(col, lanes)]
        mj = row_ref[1:2, pl.ds(col, lanes)]
        aj = row_ref[2:3, pl.ds(col, lanes)]
        before = (cj[0, 0] - mj[0, 0]).astype(jnp.int32)
        w0 = (before // 8) * 8
        slot1 = (row_in_win + (w0 + 1)).astype(F32)
        onehot = jnp.where((cj == slot1) & (mj > 0.5), 1.0, 0.0).astype(BF16)
        a1 = aj.astype(BF16)
        r1 = aj - a1.astype(F32)
        a2 = r1.astype(BF16)
        a3 = (r1 - a2.astype(F32)).astype(BF16)
        blk_row = jnp.full((1, lanes), j, F32).astype(BF16)
        vals = jnp.concatenate([lane_row, blk_row, a1, a2, a3, pad_rows], axis=0)
        y = lax.dot_general(onehot, vals, (((1,), (1,)), ((), ())), preferred_element_type=F32)
        win_ref[pl.ds(pl.multiple_of(j * win, 8), win), 0:_ROUTE_COLS] = y
        w0_ref[j] = w0
        return carry

    def place(j, carry):
        w0 = pl.multiple_of(w0_ref[j], 8)
        y = win_ref[pl.ds(pl.multiple_of(j * win, 8), win), 0:_ROUTE_COLS]
        res_ref[pl.ds(w0, win), 0:_ROUTE_COLS] = res_ref[pl.ds(w0, win), 0:_ROUTE_COLS] + y
        return carry

    lax.fori_loop(0, nblk, match, 0, unroll=min(4, nblk))
    lax.fori_loop(0, nblk, place, 0, unroll=2)
    res = res_ref[0:cap, :]
    lane = lax.broadcasted_iota(jnp.int32, res.shape, 1)
    tok = jnp.where(lane == 0, res, 0.0) + jnp.where(lane == 1, res * lanes, 0.0)
    idx_ref[...] = jnp.sum(tok, axis=1, keepdims=True).astype(jnp.int32)
    gate_ref[...] = jnp.sum(jnp.where((lane >= 2) & (lane <= 4), res, 0.0), axis=1, keepdims=True)


def _route(aff_t, col_blk0, n_set, n_sets, cap):
    n_exp = aff_t.shape[0]
    kern = functools.partial(_route_kernel, cap=cap)
    return pl.pallas_call(
        kern,
        grid=(n_sets, n_exp),
        in_specs=[pl.BlockSpec((n_exp, n_set), lambda s, e: (0, col_blk0 + s))],
        out_specs=[pl.BlockSpec((None, None, cap, 1), lambda s, e: (s, e, 0, 0)),
                   pl.BlockSpec((None, None, cap, 1), lambda s, e: (s, e, 0, 0))],
        out_shape=[jax.ShapeDtypeStruct((n_sets, n_exp, cap, 1), jnp.int32),
                   jax.ShapeDtypeStruct((n_sets, n_exp, cap, 1), F32)],
        scratch_shapes=[pltpu.VMEM((n_exp, n_set), F32), pltpu.VMEM((n_exp, n_set), F32),
                        pltpu.VMEM((8, n_set), F32), pltpu.VMEM((cap + 128 + 8, 128), F32),
                        pltpu.VMEM((n_set // 128 * (128 + 8), 128), F32),
                        pltpu.SMEM((n_set // 128,), jnp.int32)],
        compiler_params=_cparams(2),
        name="route",
    )(aff_t)


def _row_copy(src, src_row, dst, dst_row, sem):
    return pltpu.make_async_copy(src.at[pl.ds(src_row, 1), :], dst.at[pl.ds(dst_row, 1), :], sem)


def _gather_kernel(idx_ref, h_hbm, o_ref, buf, sem):
    n = buf.shape[1]
    k = pl.program_id(0)

    def fetch(r, kk):
        return _row_copy(h_hbm, idx_ref[kk * n + r], buf.at[kk % 2], r, sem.at[kk % 2])

    def each_row(fn):
        lax.fori_loop(0, n, lambda r, c: (fn(r), c)[1], 0, unroll=DMA_UNROLL)

    @pl.when(k == 0)
    def _():
        each_row(lambda r: fetch(r, k).start())

    @pl.when(k + 1 < pl.num_programs(0))
    def _():
        each_row(lambda r: fetch(r, k + 1).start())

    each_row(lambda r: fetch(r, k).wait())
    o_ref[...] = buf[k % 2].astype(o_ref.dtype)


def _gather_rows(idx, h, n=DMA_ROWS):
    r = idx.shape[0]
    n = min(n, r)
    d = h.shape[1]
    return pl.pallas_call(
        _gather_kernel,
        grid_spec=pltpu.PrefetchScalarGridSpec(
            num_scalar_prefetch=1,
            grid=(r // n,),
            in_specs=[pl.BlockSpec(memory_space=pl.ANY)],
            out_specs=pl.BlockSpec((n, d), lambda g, idx_ref: (g, 0)),
            scratch_shapes=[pltpu.VMEM((2, n, d), F32), pltpu.SemaphoreType.DMA((2,))]),
        out_shape=jax.ShapeDtypeStruct((r, d), BF16),
        compiler_params=_cparams(1),
        name="gather_rows",
    )(idx, h)


def _down_scatter_kernel(idx_ref, hid_ref, gate_ref, gt_ref, wd_hbm, x_in, x_hbm,
                         wf_ref, wb_ref, y_ref, xb_ref, wsem, rsem, osem, *, layer, tn, serial):
    del x_in
    e, i = pl.program_id(0), pl.program_id(1)
    n_e, n_i = pl.num_programs(0), pl.num_programs(1)
    n, d = y_ref.shape
    k = e * n_i + i
    slot = k % 2
    base = k * n

    def weight_copy(ee):
        return pltpu.make_async_copy(wd_hbm.at[layer, ee], wf_ref, wsem)

    def fetch(r, kk, sl):
        return _row_copy(x_hbm, idx_ref[kk * n + r], xb_ref.at[sl], r, rsem.at[sl])

    def put(r, kk, sl):
        return _row_copy(xb_ref.at[sl], r, x_hbm, idx_ref[kk * n + r], osem.at[sl])

    def each_row(fn):
        lax.fori_loop(0, n, lambda r, c: (fn(r), c)[1], 0, unroll=DMA_UNROLL)

    @pl.when(i == 0)
    def _():
        @pl.when(e == 0)
        def _():
            weight_copy(e).start()

        weight_copy(e).wait()
        wb_ref[...] = wf_ref[...].astype(BF16)

        @pl.when(e + 1 < n_e)
        def _():
            weight_copy(e + 1).start()

    if serial:
        @pl.when(k > 0)
        def _():
            each_row(lambda r: put(r, k - 1, 1 - slot).wait())

    each_row(lambda r: fetch(r, k, slot).start())
    hid = hid_ref[...]
    for p in range(d // tn):
        cs = slice(p * tn, (p + 1) * tn)
        y_ref[:, cs] = jnp.dot(hid, wb_ref[:, cs], preferred_element_type=F32) * gate_ref[...]
    each_row(lambda r: fetch(r, k, slot).wait())
    xb_ref[slot] = xb_ref[slot] + gt_ref[...] * y_ref[...]
    each_row(lambda r: put(r, k, slot).start())

    if not serial:
        @pl.when(k > 0)
        def _():
            each_row(lambda r: put(r, k - 1, 1 - slot).wait())

    @pl.when(k == n_e * n_i - 1)
    def _():
        each_row(lambda r: put(r, k, slot).wait())


def _expert_down_scatter(idx, hid, w_d, l, gates, x, mod, k_gate, rows_per_set, n_sets, mod_row,
                         tr=512, tn=1024):
    n_exp, r, ff = hid.shape
    d = w_d.shape[-1]
    tr = min(tr, rows_per_set)
    tn = min(tn, d)
    tiles_per_set = rows_per_set // tr
    assert rows_per_set % tr == 0 and r == n_sets * rows_per_set

    def gate_blk(e, i, idx_ref):
        row = (i // tiles_per_set) % n_sets if mod_row is None else mod_row
        return (row * N_MOD + k_gate, 0, 0)

    kern = functools.partial(_down_scatter_kernel, layer=l, tn=tn, serial=n_sets < 2)
    return pl.pallas_call(
        kern,
        grid_spec=pltpu.PrefetchScalarGridSpec(
            num_scalar_prefetch=1,
            grid=(n_exp, r // tr),
            in_specs=[pl.BlockSpec((None, tr, ff), lambda e, i, idx_ref: (e, i, 0)),
                      pl.BlockSpec((None, tr, 1), lambda e, i, idx_ref: (e, i, 0)),
                      pl.BlockSpec((None, 1, d), gate_blk),
                      pl.BlockSpec(memory_space=pl.ANY),
                      pl.BlockSpec(memory_space=pl.ANY)],
            out_specs=pl.BlockSpec(memory_space=pl.ANY),
            scratch_shapes=[pltpu.VMEM((ff, d), F32), pltpu.VMEM((ff, d), BF16),
                            pltpu.VMEM((tr, d), F32), pltpu.VMEM((2, tr, d), F32),
                            pltpu.SemaphoreType.DMA(()), pltpu.SemaphoreType.DMA((2,)),
                            pltpu.SemaphoreType.DMA((2,))]),
        out_shape=jax.ShapeDtypeStruct(x.shape, F32),
        input_output_aliases={5: 0},
        compiler_params=_cparams(2),
        name="expert_down_scatter",
    )(idx, hid, gates, mod, w_d, x)


def _expert_up_kernel(x_ref, wg_ref, wu_ref, o_ref, wgs_ref, wus_ref):
    first = pl.program_id(2) == 0
    _cache_weight(wg_ref, wgs_ref, first)
    _cache_weight(wu_ref, wus_ref, first)
    x = x_ref[...]
    g = jnp.dot(x, wgs_ref[...], preferred_element_type=F32)
    u = jnp.dot(x, wus_ref[...], preferred_element_type=F32)
    o_ref[...] = (g * _sigmoid(g) * u).astype(o_ref.dtype)


def _expert_up(xin, w_g, w_u, l, tr=1024, tf=256):
    n_exp, r, d = xin.shape
    ff = w_g.shape[-1]
    tr = min(tr, r)
    wspec = pl.BlockSpec((None, None, d, tf), lambda e, f, i: (l, e, 0, f))
    return pl.pallas_call(
        _expert_up_kernel,
        grid=(n_exp, ff // tf, r // tr),
        in_specs=[pl.BlockSpec((None, tr, d), lambda e, f, i: (e, i, 0)), wspec, wspec],
        out_specs=pl.BlockSpec((None, tr, tf), lambda e, f, i: (e, i, f)),
        out_shape=jax.ShapeDtypeStruct((n_exp, r, ff), BF16),
        scratch_shapes=[pltpu.VMEM((d, tf), BF16), pltpu.VMEM((d, tf), BF16)],
        compiler_params=_cparams(3),
        name="expert_up",
    )(xin, w_g, w_u)


def _moe(x, h2, aff_t, mod, k_gate, w_g, w_u, w_d, l, row0, n_set, n_sets, mod_row):
    n_exp = aff_t.shape[0]
    d = x.shape[1]
    cap = CAPACITY_FACTOR * n_set // n_exp
    idx, gates = _route(aff_t, row0 // n_set, n_set, n_sets, cap)
    offs = row0 + n_set * jnp.arange(n_sets, dtype=jnp.int32)
    rows = (idx.reshape(n_sets, n_exp, cap) + offs[:, None, None]).transpose(1, 0, 2).reshape(-1)
    gates = gates.reshape(n_sets, n_exp, cap).transpose(1, 0, 2).reshape(n_exp, n_sets * cap, 1)
    xin = _gather_rows(rows, h2).reshape(n_exp, n_sets * cap, d)
    hid = _expert_up(xin, w_g, w_u, l)
    return _expert_down_scatter(rows, hid, w_d, l, gates, x, mod, k_gate, cap, n_sets, mod_row)


def kernel(x, c, ctx, c_ctx, w_ada, b_ada, g_norm1, w_in, w_gate2, b_gate, g_head, w_branch_a,
           w_branch_b, w_out, g_norm2, w_router, w_exp_gate, w_exp_up, w_exp_down, g_final):
    bsz, n_lat, d = x.shape
    n_ctx = ctx.shape[1]
    depth = w_ada.shape[0]
    fw = w_branch_a.shape[1]
    gk = w_gate2.shape[3]
    gv = g_head.shape[1]
    rank = w_gate2.shape[2]
    dk, dv = gk // GLA_HEADS, gv // GLA_HEADS
    cgrp = fw // N_FOURIER_GROUPS
    lat_rows = bsz * n_lat
    q_col, k_col, v_col, r_col = fw, fw + gk, fw + 2 * gk, fw + 2 * gk + gv
    lr_col = r_col + gv
    gl_col = lr_col + N_DIR * rank
    lr_w = 128

    xs, xs_ctx = x.reshape(lat_rows, d), ctx.reshape(bsz * n_ctx, d)
    c_rows = jnp.concatenate([c, c_ctx[None, :], jnp.zeros((8 - bsz - 1, d), F32)], axis=0)
    mod_all = _adaln(c_rows, w_ada, b_ada)
    wc, m1, m2, mc = _dft_tables(n_lat, n_ctx, cgrp)
    zero_state = jnp.zeros((bsz, GLA_HEADS, N_DIR, dv, dk), F32)
    w_in_t = jnp.swapaxes(w_in, 1, 2)

    for l in range(depth):
        last = l == depth - 1
        mod = mod_all[l].reshape(8 * N_MOD, 1, d)
        h = _norm_mod(xs, g_norm1[l][None, :], mod, 1, 0, n_lat, bsz, BF16, x_ctx=xs_ctx)
        tm_in = 768 if h.shape[0] % 768 == 0 else 512
        z = _matmul_t(h, w_in_t, l, 0, lr_col, tm=tm_in)
        lr = _matmul_t(h, w_in_t, l, lr_col, lr_w)
        gl = _matmul_t(h, w_in_t, l, gl_col, 2 * d, tm=tm_in)
        p, q = _chan_dft(z, wc, fw)
        four_l = _seq_dft(p, q, m1, m2, n_lat, bsz, 1.0 / math.sqrt(n_lat * cgrp))
        four_c = _ctx_dft(p, q, mc, lat_rows // n_ctx, n_ctx, bsz, 1.0 / math.sqrt(n_ctx * cgrp))
        of_c, ob_c, s_c = _gla(z, lr, w_gate2[l], b_gate[l], zero_state, lat_rows, n_ctx, bsz,
                               q_col, k_col, v_col, dk, dv)
        of_l, ob_l, _ = _gla(z, lr, w_gate2[l], b_gate[l], s_c, 0, n_lat, bsz,
                             q_col, k_col, v_col, dk, dv)
        ogla_l = _gla_post(of_l, ob_l, z, 0, g_head[l][None, :], r_col, dv)
        ogla_c = _gla_post(of_c, ob_c, z, lat_rows, g_head[l][None, :], r_col, dv)
        m = _merge(four_l, four_c, ogla_l, ogla_c, w_branch_a, w_branch_b, l, gl)
        xs = _out_residual(m, w_out, l, xs, mod, 2, n_lat, bsz, x_ctx=xs_ctx)
        xs_ctx = None
        wr_t = w_router[l].T.astype(BF16)
        h2, aff_t = _norm_mod_router(xs, g_norm2[l][None, :], mod, 4, 3, wr_t, n_lat, bsz)
        xs = _moe(xs, h2, aff_t, mod, 5, w_exp_gate, w_exp_up, w_exp_down, l,
                  0, n_lat, bsz, None)
        if not last:
            xs = _moe(xs, h2, aff_t, mod, 5, w_exp_gate, w_exp_up, w_exp_down, l,
                      lat_rows, n_ctx, bsz, bsz)
    return _final_norm(xs, g_final[None, :], lat_rows).reshape(bsz, n_lat, d)
```
